```python
import math
import jax, jax.numpy as jnp
from jax import lax
import numpy as np

D_MODEL = 2048
BATCH = 4
SEQ = 2048
DEPTH = 2
DEC_BATCH = 32
DEC_SEQ = 8
PAST_LEN = 8192
PAGE_SIZE = 128

N_HEADS = 8
HEAD_DIM = 128
KV_HEADS = 2
Q_PER_KV = N_HEADS // KV_HEADS
ATT_W = N_HEADS * HEAD_DIM
KV_COLS = KV_HEADS * HEAD_DIM
CMP_BLK = 32
CMP_STRIDE = 16
CMP_HID = HEAD_DIM
SEL_BLK = 64
N_SEL_TOP = 16
WINDOW = 512
Q_BLOCK = 64
FORCE_BONUS = 1000.0
NEG = -1e30
ROPE_THETA = 10000.0
ATT_SCALE = HEAD_DIM ** -0.5
C_CONV = D_MODEL - ATT_W
CONV_W = 31
MIX_W = ATT_W + C_CONV
IN_COLS = ATT_W + 6 * KV_COLS + 3 * N_HEADS + 2 * C_CONV
SPLITS = (ATT_W, ATT_W + 2 * KV_COLS, ATT_W + 4 * KV_COLS, ATT_W + 6 * KV_COLS,
          ATT_W + 6 * KV_COLS + 3 * N_HEADS)
N_EXPERTS = 32
TOP_K = 4
D_FF = 2048
SWIGLU_LIMIT = 7.0
SWIGLU_ALPHA = 1.702
MOE_BLOCK = 128
LN_EPS = 1e-5
DN_ALPHA = (2 * DEPTH) ** 0.25
DN_BETA = (8 * DEPTH) ** -0.25

kernel_name = "hymba_nsa_conformer_moe_step"


def layer_norm(x, g, b):
    x32 = x.astype(jnp.float32)
    mu = jnp.mean(x32, axis=-1, keepdims=True)
    var = jnp.mean(jnp.square(x32 - mu), axis=-1, keepdims=True)
    return ((x32 - mu) * lax.rsqrt(var + LN_EPS) * g + b).astype(x.dtype)


def rope(x, pos):
    inv = ROPE_THETA ** (-jnp.arange(0, HEAD_DIM, 2, dtype=jnp.float32) / HEAD_DIM)
    ang = pos.astype(jnp.float32)[:, None] * inv[None, :]
    cos = jnp.cos(ang)[:, None, :]
    sin = jnp.sin(ang)[:, None, :]
    x32 = x.astype(jnp.float32)
    x1, x2 = x32[..., :HEAD_DIM // 2], x32[..., HEAD_DIM // 2:]
    return jnp.concatenate([x1 * cos - x2 * sin, x2 * cos + x1 * sin], axis=-1).astype(x.dtype)


def in_proj(x, pos, w_in):
    B, T, _ = x.shape
    h = jnp.einsum('btd,dc->btc', x, w_in)
    q, kvc, kvs, kvw, g, cin = jnp.split(h, SPLITS, axis=-1)
    q = rope(q.reshape(B, T, N_HEADS, HEAD_DIM), pos)

    def kv(a):
        a = a.reshape(B, T, 2, KV_HEADS, HEAD_DIM)
        return jnp.stack([rope(a[:, :, 0], pos), a[:, :, 1]], axis=2)

    gates = jax.nn.sigmoid(g.astype(jnp.float32)).reshape(B, T, N_HEADS, 3)
    return q, kv(kvc), kv(kvs), kv(kvw), gates, cin


def compress(kv_rows, pe, w1, b1, w2, b2):
    B, L = kv_rows.shape[:2]
    R = CMP_BLK // CMP_STRIDE
    n_ch = L // CMP_STRIDE
    n_cmp = n_ch - R + 1
    ch = kv_rows[:, :n_ch * CMP_STRIDE].reshape(B, n_ch, CMP_STRIDE, 2, KV_HEADS, HEAD_DIM)
    w1r = w1.reshape(2, R, CMP_STRIDE, HEAD_DIM, CMP_HID)
    proj = jnp.einsum('bcskgd,krsdh->rbckgh', ch, w1r)
    z = proj[0, :, :n_cmp]
    for r in range(1, R):
        z = z + proj[r, :, r:r + n_cmp]
    z = z + (b1 + jnp.einsum('kld,kldh->kh', pe, w1))[:, None, :]
    c = jnp.einsum('bckgh,khd->bckgd', jax.nn.gelu(z), w2) + b2[:, None, :]
    c_end = np.arange(n_cmp) * CMP_STRIDE + CMP_BLK - 1
    return c[:, :, 0], c[:, :, 1], jnp.asarray(c_end, dtype=jnp.int32)


def masked_softmax(s, mask):
    s32 = jnp.where(mask, s.astype(jnp.float32), NEG)
    return jax.nn.softmax(s32, axis=-1) * mask


def block_overlap(n_cmp, n_sel):
    i = np.arange(n_cmp)[:, None]
    j = np.arange(n_sel)[None, :]
    ov = (i * CMP_STRIDE < (j + 1) * SEL_BLK) & (i * CMP_STRIDE + CMP_BLK > j * SEL_BLK)
    return jnp.asarray(ov.astype(np.float32))


def gather_blocks(rows, idx):
    B, L = rows.shape[:2]
    rb = rows.reshape(B, L // SEL_BLK, SEL_BLK, KV_HEADS, HEAD_DIM).transpose(0, 3, 1, 2, 4)
    ig = idx.transpose(0, 2, 1, 3)
    return jax.vmap(jax.vmap(lambda r, i: r[i]))(rb, ig)


def nsa_core(q, q_pos, gates, kc, vc, c_end, ks, vs, kw, vw, w_pos):
    B, Tq = q.shape[:2]
    qg = q.reshape(B, Tq, KV_HEADS, Q_PER_KV, HEAD_DIM)
    s_c = jnp.einsum('btgrd,bcgd->btgrc', qg, kc) * ATT_SCALE
    m_c = (c_end[None, :] <= q_pos[:, None])[None, :, None, None, :]
    p_c = masked_softmax(s_c, m_c)
    o_c = jnp.einsum('btgrc,bcgd->btgrd', p_c, vc)
    n_sel = ks.shape[1] // SEL_BLK
    imp = jnp.einsum('btgrc,cs->btgs', p_c, block_overlap(kc.shape[1], n_sel))
    j = jnp.arange(n_sel)
    cur = q_pos // SEL_BLK
    valid = j[None, :] * SEL_BLK <= q_pos[:, None]
    forced = (j[None, :] == 0) | (j[None, :] == cur[:, None]) | (j[None, :] == cur[:, None] - 1)
    score = jnp.where(valid[None, :, None, :], imp + FORCE_BONUS * forced[None, :, None, :].astype(jnp.float32), NEG)
    n_top = min(N_SEL_TOP, n_sel)
    top_val, top_idx = lax.top_k(score, n_top)
    ok = top_val > NEG / 2
    kg = gather_blocks(ks, top_idx)
    vg = gather_blocks(vs, top_idx)
    s_s = jnp.einsum('btgrd,bgtnsd->btgrns', qg, kg) * ATT_SCALE
    k_pos = top_idx[..., None] * SEL_BLK + jnp.arange(SEL_BLK)
    m_s = (k_pos <= q_pos[None, :, None, None, None]) & ok[..., None]
    flat = n_top * SEL_BLK
    p_s = masked_softmax(s_s.reshape(B, Tq, KV_HEADS, Q_PER_KV, flat),
                         m_s[:, :, :, None].reshape(B, Tq, KV_HEADS, 1, flat))
    o_s = jnp.einsum('btgrns,bgtnsd->btgrd', p_s.reshape(s_s.shape), vg)
    s_w = jnp.einsum('btgrd,bkgd->btgrk', qg, kw) * ATT_SCALE
    delta = q_pos[:, None] - w_pos[None, :]
    m_w = ((delta >= 0) & (delta < WINDOW) & (w_pos[None, :] >= 0))[None, :, None, None, :]
    o_w = jnp.einsum('btgrk,bkgd->btgrd', masked_softmax(s_w, m_w), vw)
    g = gates.reshape(B, Tq, KV_HEADS, Q_PER_KV, 3)
    o = g[..., 0:1] * o_c + g[..., 1:2] * o_s + g[..., 2:3] * o_w
    return o.reshape(B, Tq, ATT_W).astype(q.dtype)


def nsa_prompt(q, gates, kc, vc, c_end, kv_sel, kw_pad):
    B, T = q.shape[:2]
    ks, vs = kv_sel[:, :, 0], kv_sel[:, :, 1]

    def blk(i):
        s = i * Q_BLOCK
        qb = lax.dynamic_slice_in_dim(q, s, Q_BLOCK, axis=1)
        gb = lax.dynamic_slice_in_dim(gates, s, Q_BLOCK, axis=1)
        kwb = lax.dynamic_slice_in_dim(kw_pad, s, WINDOW + Q_BLOCK, axis=1)
        q_pos = s + jnp.arange(Q_BLOCK)
        w_pos = s - WINDOW + jnp.arange(WINDOW + Q_BLOCK)
        return nsa_core(qb, q_pos, gb, kc, vc, c_end, ks, vs, kwb[:, :, 0], kwb[:, :, 1], w_pos)

    out = lax.map(blk, jnp.arange(T // Q_BLOCK))
    return out.transpose(1, 0, 2, 3).reshape(B, T, ATT_W)


def conv_module(cin, buf, conv_w, conv_b, ln_g, ln_b):
    a, gt = jnp.split(cin, 2, axis=-1)
    u = a * jax.nn.sigmoid(gt)
    seq = jnp.concatenate([buf.astype(u.dtype), u], axis=1)
    y = lax.conv_general_dilated(seq, conv_w[:, None, :].astype(seq.dtype), window_strides=(1,),
                                 padding='VALID', dimension_numbers=('NWC', 'WIO', 'NWC'),
                                 feature_group_count=C_CONV) + conv_b
    y = layer_norm(y, ln_g, ln_b)
    return jax.nn.silu(y), seq[:, -(CONV_W - 1):]


def moe(x, router_w, router_b, w_up, b_up, w_down, b_down):
    B, T, D = x.shape
    N = B * T
    xt = x.reshape(N, D)
    logits = (xt @ router_w + router_b).astype(jnp.float32)
    top_v, top_e = lax.top_k(logits, TOP_K)
    gate = jax.nn.softmax(top_v, axis=-1)
    M = N * TOP_K
    flat_e = top_e.reshape(M)
    flat_tok = jnp.repeat(jnp.arange(N, dtype=jnp.int32), TOP_K)
    flat_w = gate.reshape(M)
    order = jnp.argsort(flat_e)
    se = flat_e[order]
    counts = jnp.zeros((N_EXPERTS,), jnp.int32).at[flat_e].add(1)
    padded = (counts + MOE_BLOCK - 1) // MOE_BLOCK * MOE_BLOCK
    start = jnp.cumsum(counts) - counts
    pend = jnp.cumsum(padded)
    pstart = pend - padded
    dest = pstart[se] + (jnp.arange(M, dtype=jnp.int32) - start[se])
    n_blocks = -(-(M + N_EXPERTS * (MOE_BLOCK - 1)) // MOE_BLOCK)
    rows = n_blocks * MOE_BLOCK
    row_tok = jnp.full((rows,), N, jnp.int32).at[dest].set(flat_tok[order])
    row_w = jnp.zeros((rows,), jnp.float32).at[dest].set(flat_w[order])
    block_e = jnp.clip(jnp.searchsorted(pend, jnp.arange(n_blocks, dtype=jnp.int32) * MOE_BLOCK, side='right'),
                       0, N_EXPERTS - 1)
    x_pad = jnp.concatenate([xt, jnp.zeros((1, D), xt.dtype)], axis=0)

    def run_block(args):
        tok, e = args
        h = x_pad[tok] @ w_up[e] + b_up[e]
        h_glu = jnp.minimum(h[:, ::2], SWIGLU_LIMIT)
        h_lin = jnp.clip(h[:, 1::2], -SWIGLU_LIMIT, SWIGLU_LIMIT)
        act = h_glu * jax.nn.sigmoid(SWIGLU_ALPHA * h_glu) * (h_lin + 1.0)
        return act @ w_down[e] + b_down[e]

    out = lax.map(run_block, (row_tok.reshape(n_blocks, MOE_BLOCK), block_e))
    out = out.reshape(rows, D) * row_w[:, None]
    y = jax.ops.segment_sum(out, row_tok, num_segments=N + 1)[:N]
    return y.reshape(B, T, D).astype(x.dtype)


def post_mixer(x, att, conv_out, w_out, ln1_g, ln1_b, router_w, router_b, w_up, b_up, w_down, b_down, ln2_g, ln2_b):
    mix = jnp.einsum('btc,cd->btd', jnp.concatenate([att, conv_out.astype(att.dtype)], axis=-1), w_out)
    x = layer_norm(DN_ALPHA * x + mix, ln1_g, ln1_b)
    return layer_norm(DN_ALPHA * x + moe(x, router_w, router_b, w_up, b_up, w_down, b_down), ln2_g, ln2_b)


def setup_inputs(seed: int = 0) -> dict:
    key = jax.random.key(seed)
    ks = jax.random.split(key, 28)
    f32 = jnp.float32

    def nrm(k, shape, scale):
        return jax.random.normal(k, shape, f32) * scale

    n_pages = PAST_LEN // PAGE_SIZE
    n_used = DEC_BATCH * n_pages
    n_pool = n_used + n_used // 4
    pool_shape = (DEPTH, n_pool, PAGE_SIZE, 2, KV_HEADS, HEAD_DIM)
    page_table = jax.random.permutation(ks[6], n_pool)[:n_used].reshape(DEC_BATCH, n_pages).astype(jnp.int32)
    return {
        "x_prompt": nrm(ks[0], (BATCH, SEQ, D_MODEL), 1.0),
        "x_sample": nrm(ks[1], (DEC_BATCH, DEC_SEQ, D_MODEL), 1.0),
        "cache_cmp_kv": nrm(ks[2], pool_shape, 1.0),
        "cache_sel_kv": nrm(ks[3], pool_shape, 1.0),
        "state_win_kv": nrm(ks[4], (DEPTH, DEC_BATCH, WINDOW, 2, KV_HEADS, HEAD_DIM), 1.0),
        "state_conv": nrm(ks[5], (DEPTH, DEC_BATCH, CONV_W - 1, C_CONV), 0.5),
        "page_table": page_table,
        "w_in": nrm(ks[7], (DEPTH, D_MODEL, IN_COLS), D_MODEL ** -0.5),
        "cmp_pe": nrm(ks[8], (DEPTH, 2, CMP_BLK, HEAD_DIM), 0.1),
        "cmp_w1": nrm(ks[9], (DEPTH, 2, CMP_BLK, HEAD_DIM, CMP_HID), (CMP_BLK * HEAD_DIM) ** -0.5),
        "cmp_b1": nrm(ks[10], (DEPTH, 2, CMP_HID), 0.01),
        "cmp_w2": nrm(ks[11], (DEPTH, 2, CMP_HID, HEAD_DIM), CMP_HID ** -0.5),
        "cmp_b2": nrm(ks[12], (DEPTH, 2, HEAD_DIM), 0.01),
        "conv_w": nrm(ks[13], (DEPTH, CONV_W, C_CONV), CONV_W ** -0.5),
        "conv_b": nrm(ks[14], (DEPTH, C_CONV), 0.01),
        "conv_ln_g": 1.0 + nrm(ks[15], (DEPTH, C_CONV), 0.01),
        "conv_ln_b": nrm(ks[16], (DEPTH, C_CONV), 0.01),
        "w_out": nrm(ks[17], (DEPTH, MIX_W, D_MODEL), MIX_W ** -0.5 * DN_BETA),
        "ln1_g": 1.0 + nrm(ks[18], (DEPTH, D_MODEL), 0.01),
        "ln1_b": nrm(ks[19], (DEPTH, D_MODEL), 0.01),
        "router_w": nrm(ks[20], (DEPTH, D_MODEL, N_EXPERTS), D_MODEL ** -0.5),
        "router_b": nrm(ks[21], (DEPTH, N_EXPERTS), 0.01),
        "exp_w_up": nrm(ks[22], (DEPTH, N_EXPERTS, D_MODEL, 2 * D_FF), D_MODEL ** -0.5),
        "exp_b_up": nrm(ks[23], (DEPTH, N_EXPERTS, 2 * D_FF), 0.01),
        "exp_w_down": nrm(ks[24], (DEPTH, N_EXPERTS, D_FF, D_MODEL), D_FF ** -0.5 * DN_BETA),
        "exp_b_down": nrm(ks[25], (DEPTH, N_EXPERTS, D_MODEL), 0.01),
        "ln2_g": 1.0 + nrm(ks[26], (DEPTH, D_MODEL), 0.01),
        "ln2_b": nrm(ks[27], (DEPTH, D_MODEL), 0.01),
    }


def reference(x_prompt, x_sample, cache_cmp_kv, cache_sel_kv, state_win_kv, state_conv, page_table,
              w_in, cmp_pe, cmp_w1, cmp_b1, cmp_w2, cmp_b2, conv_w, conv_b, conv_ln_g, conv_ln_b,
              w_out, ln1_g, ln1_b, router_w, router_b, exp_w_up, exp_b_up, exp_w_down, exp_b_down,
              ln2_g, ln2_b):
    B, T, _ = x_prompt.shape
    DB, DS, _ = x_sample.shape
    n_pages = page_table.shape[1]
    past = n_pages * PAGE_SIZE
    pos_p = jnp.arange(T, dtype=jnp.int32)
    pos_s = past + jnp.arange(DS, dtype=jnp.int32)
    xp, xs = x_prompt, x_sample
    p_cmp, p_sel, p_win, p_conv = [], [], [], []
    s_cmp, s_sel, s_win, s_conv = [], [], [], []
    for l in range(DEPTH):
        q, kvc, kvs, kvw, g, cin = in_proj(xp, pos_p, w_in[l])
        kc, vc, c_end = compress(kvc, cmp_pe[l], cmp_w1[l], cmp_b1[l], cmp_w2[l], cmp_b2[l])
        kw_pad = jnp.concatenate([jnp.zeros((B, WINDOW, 2, KV_HEADS, HEAD_DIM), kvw.dtype), kvw], axis=1)
        att = nsa_prompt(q, g, kc, vc, c_end, kvs, kw_pad)
        conv_out, cbuf = conv_module(cin, jnp.zeros((B, CONV_W - 1, C_CONV), cin.dtype),
                                     conv_w[l], conv_b[l], conv_ln_g[l], conv_ln_b[l])
        xp = post_mixer(xp, att, conv_out, w_out[l], ln1_g[l], ln1_b[l], router_w[l], router_b[l],
                        exp_w_up[l], exp_b_up[l], exp_w_down[l], exp_b_down[l], ln2_g[l], ln2_b[l])
        p_cmp.append(kvc)
        p_sel.append(kvs)
        p_win.append(kw_pad[:, -WINDOW:])
        p_conv.append(cbuf)
        q, kvc, kvs, kvw, g, cin = in_proj(xs, pos_s, w_in[l])
        cmp_rows = jnp.concatenate(
            [cache_cmp_kv[l][page_table].reshape(DB, past, 2, KV_HEADS, HEAD_DIM), kvc.astype(cache_cmp_kv.dtype)], axis=1)
        kc, vc, c_end = compress(cmp_rows, cmp_pe[l], cmp_w1[l], cmp_b1[l], cmp_w2[l], cmp_b2[l])
        sel_rows = jnp.concatenate(
            [cache_sel_kv[l][page_table].reshape(DB, past, 2, KV_HEADS, HEAD_DIM), kvs.astype(cache_sel_kv.dtype)], axis=1)
        pad = (-sel_rows.shape[1]) % SEL_BLK
        sel_rows = jnp.pad(sel_rows, ((0, 0), (0, pad), (0, 0), (0, 0), (0, 0)))
        kw = jnp.concatenate([state_win_kv[l], kvw.astype(state_win_kv.dtype)], axis=1)
        w_pos = past - WINDOW + jnp.arange(WINDOW + DS, dtype=jnp.int32)
        att = nsa_core(q, pos_s, g, kc, vc, c_end, sel_rows[:, :, 0], sel_rows[:, :, 1],
                       kw[:, :, 0], kw[:, :, 1], w_pos)
        conv_out, cbuf = conv_module(cin, state_conv[l], conv_w[l], conv_b[l], conv_ln_g[l], conv_ln_b[l])
        xs = post_mixer(xs, att, conv_out, w_out[l], ln1_g[l], ln1_b[l], router_w[l], router_b[l],
                        exp_w_up[l], exp_b_up[l], exp_w_down[l], exp_b_down[l], ln2_g[l], ln2_b[l])
        s_cmp.append(kvc)
        s_sel.append(kvs)
        s_win.append(kw[:, -WINDOW:])
        s_conv.append(cbuf)
    return (xp, xs,
            jnp.stack(p_cmp), jnp.stack(p_sel), jnp.stack(p_win), jnp.stack(p_conv),
            jnp.stack(s_cmp), jnp.stack(s_sel), jnp.stack(s_win), jnp.stack(s_conv))
```

```python
import functools
import math

import jax
import jax.numpy as jnp
import numpy as np
from jax import lax
from jax.experimental import pallas as pl
from jax.experimental.pallas import tpu as pltpu

D_MODEL = 2048
DEPTH = 2
PAGE_SIZE = 128
N_HEADS = 8
HEAD_DIM = 128
KV_HEADS = 2
Q_PER_KV = N_HEADS // KV_HEADS
ATT_W = N_HEADS * HEAD_DIM
KV_COLS = KV_HEADS * HEAD_DIM
CMP_BLK = 32
CMP_STRIDE = 16
CMP_HID = HEAD_DIM
SEL_BLK = 64
N_SEL_TOP = 16
WINDOW = 512
FORCE_BONUS = 1000.0
NEG = -1e30
ROPE_THETA = 10000.0
ATT_SCALE = HEAD_DIM ** -0.5
C_CONV = D_MODEL - ATT_W
CONV_W = 31
MIX_W = ATT_W + C_CONV
QKV_COLS = ATT_W + 6 * KV_COLS
GATE_COLS = 3 * N_HEADS
N_EXPERTS = 32
TOP_K = 4
D_FF = 2048
SWIGLU_LIMIT = 7.0
SWIGLU_ALPHA = 1.702
LN_EPS = 1e-5
DN_ALPHA = (2 * DEPTH) ** 0.25

LANES = 128
VMEM_LIMIT = 48 * 1024 * 1024
MOE_TM = 256
HI = lax.Precision.HIGHEST


def _mm_kernel(a_ref, w_ref, o_ref, wbf_ref):
    @pl.when(pl.program_id(1) == 0)
    def _():
        wbf_ref[...] = w_ref[...].astype(jnp.bfloat16)

    o_ref[...] = jnp.dot(a_ref[...], wbf_ref[...], preferred_element_type=jnp.float32)


def _matmul(a, w, tm, tn):
    M, K = a.shape
    N = w.shape[1]
    assert M % tm == 0 and N % tn == 0
    return pl.pallas_call(
        _mm_kernel,
        grid=(N // tn, M // tm),
        in_specs=[pl.BlockSpec((tm, K), lambda j, i: (i, 0)),
                  pl.BlockSpec((K, tn), lambda j, i: (0, j))],
        out_specs=pl.BlockSpec((tm, tn), lambda j, i: (i, j)),
        out_shape=jax.ShapeDtypeStruct((M, N), jnp.float32),
        scratch_shapes=[pltpu.VMEM((K, tn), jnp.bfloat16)],
        compiler_params=pltpu.CompilerParams(dimension_semantics=("arbitrary", "arbitrary"),
                                             vmem_limit_bytes=VMEM_LIMIT),
        name="dense_matmul",
    )(a, w)


def _moe_up_kernel(be_ref, bv_ref, x_ref, wg_ref, wl_ref, bg_ref, bl_ref, o_ref):
    i = pl.program_id(1)

    @pl.when(bv_ref[i] != 0)
    def _():
        x = x_ref[...]
        hg = jnp.dot(x, wg_ref[...], preferred_element_type=jnp.float32) + bg_ref[...]
        hl = jnp.dot(x, wl_ref[...], preferred_element_type=jnp.float32) + bl_ref[...]
        g = jnp.minimum(hg, SWIGLU_LIMIT)
        lin = jnp.clip(hl, -SWIGLU_LIMIT, SWIGLU_LIMIT)
        o_ref[...] = (g * jax.nn.sigmoid(SWIGLU_ALPHA * g) * (lin + 1.0)).astype(o_ref.dtype)

    @pl.when(bv_ref[i] == 0)
    def _():
        o_ref[...] = jnp.zeros_like(o_ref)


def _moe_up(block_e, block_valid, xs, w_glu, w_lin, b_glu, b_lin, tn):
    rows, D = xs.shape
    F = w_glu.shape[2]
    nb = rows // MOE_TM
    grid_spec = pltpu.PrefetchScalarGridSpec(
        num_scalar_prefetch=2,
        grid=(F // tn, nb),
        in_specs=[pl.BlockSpec((MOE_TM, D), lambda j, i, be, bv: (i, 0)),
                  pl.BlockSpec((None, D, tn), lambda j, i, be, bv: (be[i], 0, j)),
                  pl.BlockSpec((None, D, tn), lambda j, i, be, bv: (be[i], 0, j)),
                  pl.BlockSpec((None, 1, tn), lambda j, i, be, bv: (be[i], 0, j)),
                  pl.BlockSpec((None, 1, tn), lambda j, i, be, bv: (be[i], 0, j))],
        out_specs=pl.BlockSpec((MOE_TM, tn), lambda j, i, be, bv: (i, j)),
    )
    return pl.pallas_call(
        _moe_up_kernel,
        grid_spec=grid_spec,
        out_shape=jax.ShapeDtypeStruct((rows, F), jnp.bfloat16),
        compiler_params=pltpu.CompilerParams(dimension_semantics=("arbitrary", "arbitrary"),
                                             vmem_limit_bytes=VMEM_LIMIT),
        name="moe_up",
    )(block_e, block_valid, xs, w_glu, w_lin, b_glu, b_lin)


def _moe_down_kernel(be_ref, bv_ref, a_ref, w_ref, b_ref, rw_ref, o_ref, wbf_ref):
    i = pl.program_id(1)
    prev_e = be_ref[jnp.maximum(i - 1, 0)]

    @pl.when((i == 0) | (be_ref[i] != prev_e))
    def _():
        wbf_ref[...] = w_ref[...].astype(jnp.bfloat16)

    @pl.when(bv_ref[i] != 0)
    def _():
        h = jnp.dot(a_ref[...], wbf_ref[...], preferred_element_type=jnp.float32) + b_ref[...]
        o_ref[...] = h * rw_ref[...]

    @pl.when(bv_ref[i] == 0)
    def _():
        o_ref[...] = jnp.zeros_like(o_ref)


def _moe_down(block_e, block_valid, act, w_down, b_down, row_w, tn):
    rows, F = act.shape
    D = w_down.shape[2]
    nb = rows // MOE_TM
    grid_spec = pltpu.PrefetchScalarGridSpec(
        num_scalar_prefetch=2,
        grid=(D // tn, nb),
        in_specs=[pl.BlockSpec((MOE_TM, F), lambda j, i, be, bv: (i, 0)),
                  pl.BlockSpec((None, F, tn), lambda j, i, be, bv: (be[i], 0, j)),
                  pl.BlockSpec((None, 1, tn), lambda j, i, be, bv: (be[i], 0, j)),
                  pl.BlockSpec((MOE_TM, 1), lambda j, i, be, bv: (i, 0))],
        out_specs=pl.BlockSpec((MOE_TM, tn), lambda j, i, be, bv: (i, j)),
        scratch_shapes=[pltpu.VMEM((F, tn), jnp.bfloat16)],
    )
    return pl.pallas_call(
        _moe_down_kernel,
        grid_spec=grid_spec,
        out_shape=jax.ShapeDtypeStruct((rows, D), jnp.float32),
        compiler_params=pltpu.CompilerParams(dimension_semantics=("arbitrary", "arbitrary"),
                                             vmem_limit_bytes=VMEM_LIMIT),
        name="moe_down",
    )(block_e, block_valid, act, w_down, b_down, row_w)


def _moe(x, router_w, router_b, w_up, b_up, w_down, b_down):
    N, D = x.shape
    logits = jnp.dot(x, router_w, precision=HI) + router_b
    top_v, top_e = lax.top_k(logits, TOP_K)
    gate = jax.nn.softmax(top_v, axis=-1)
    M = N * TOP_K
    flat_e = top_e.reshape(M)
    order = jnp.argsort(flat_e)
    se = flat_e[order]
    counts = jnp.sum(flat_e[:, None] == jnp.arange(N_EXPERTS, dtype=flat_e.dtype)[None, :], axis=0).astype(jnp.int32)
    padded = (counts + MOE_TM - 1) // MOE_TM * MOE_TM
    start = jnp.cumsum(counts) - counts
    pend = jnp.cumsum(padded)
    pstart = pend - padded
    dest = pstart[se] + (jnp.arange(M, dtype=jnp.int32) - start[se])
    nb = -(-(M + N_EXPERTS * (MOE_TM - 1)) // MOE_TM)
    rows = nb * MOE_TM
    row_tok = jnp.full((rows,), N, jnp.int32).at[dest].set((order // TOP_K).astype(jnp.int32))
    row_w = jnp.zeros((rows,), jnp.float32).at[dest].set(gate.reshape(M)[order])
    slot = jnp.zeros((M,), jnp.int32).at[order].set(dest)
    blk0 = jnp.arange(nb, dtype=jnp.int32) * MOE_TM
    block_e = jnp.clip(jnp.searchsorted(pend, blk0, side='right'), 0, N_EXPERTS - 1).astype(jnp.int32)
    block_valid = (blk0 < pend[-1]).astype(jnp.int32)

    x_pad = jnp.concatenate([x.astype(jnp.bfloat16), jnp.zeros((1, D), jnp.bfloat16)], axis=0)
    xs = x_pad[row_tok]
    w_up_pair = w_up.reshape(N_EXPERTS, D, D_FF, 2)
    w_glu = w_up_pair[..., 0].astype(jnp.bfloat16)
    w_lin = w_up_pair[..., 1].astype(jnp.bfloat16)
    b_up_pair = b_up.reshape(N_EXPERTS, 1, D_FF, 2)
    act = _moe_up(block_e, block_valid, xs, w_glu, w_lin, b_up_pair[..., 0], b_up_pair[..., 1], tn=1024)
    out = _moe_down(block_e, block_valid, act, w_down, b_down.reshape(N_EXPERTS, 1, D), row_w[:, None], tn=1024)
    return jnp.sum(out[slot.reshape(N, TOP_K)], axis=1)


def _layer_norm(x, g, b):
    mu = jnp.mean(x, axis=-1, keepdims=True)
    var = jnp.mean(jnp.square(x - mu), axis=-1, keepdims=True)
    return (x - mu) * lax.rsqrt(var + LN_EPS) * g + b


def _rope(x, pos):
    inv = ROPE_THETA ** (-jnp.arange(0, HEAD_DIM, 2, dtype=jnp.float32) / HEAD_DIM)
    ang = pos.astype(jnp.float32)[:, None] * inv[None, :]
    cos = jnp.cos(ang)[:, None, :]
    sin = jnp.sin(ang)[:, None, :]
    x1, x2 = x[..., :HEAD_DIM // 2], x[..., HEAD_DIM // 2:]
    return jnp.concatenate([x1 * cos - x2 * sin, x2 * cos + x1 * sin], axis=-1)


def _masked_softmax(s, mask):
    s = jnp.where(mask, s, NEG)
    m = jnp.max(s, axis=-1, keepdims=True)
    e = jnp.where(mask, jnp.exp(s - m), 0.0)
    return e / jnp.maximum(jnp.sum(e, axis=-1, keepdims=True), 1e-30)


def _block_overlap(n_cmp, n_sel):
    i = np.arange(n_cmp)[:, None]
    j = np.arange(n_sel)[None, :]
    ov = (i * CMP_STRIDE < (j + 1) * SEL_BLK) & (i * CMP_STRIDE + CMP_BLK > j * SEL_BLK)
    return jnp.asarray(ov.astype(np.float32))


def _compress(kv_rows, pe, w1, b1, w2, b2):
    B, L = kv_rows.shape[:2]
    R = CMP_BLK // CMP_STRIDE
    n_ch = L // CMP_STRIDE
    n_cmp = n_ch - R + 1
    ch = kv_rows[:, :n_ch * CMP_STRIDE].reshape(B, n_ch, CMP_STRIDE, 2, KV_HEADS, HEAD_DIM)
    w1r = w1.reshape(2, R, CMP_STRIDE, HEAD_DIM, CMP_HID)
    proj = jnp.einsum('bcskgd,krsdh->rbckgh', ch, w1r)
    z = proj[0, :, :n_cmp]
    for r in range(1, R):
        z = z + proj[r, :, r:r + n_cmp]
    z = z + (b1 + jnp.einsum('kld,kldh->kh', pe, w1, precision=HI))[:, None, :]
    c = jnp.einsum('bckgh,khd->bckgd', jax.nn.gelu(z), w2) + b2[:, None, :]
    c_end = np.arange(n_cmp) * CMP_STRIDE + CMP_BLK - 1
    return c[:, :, 0], c[:, :, 1], jnp.asarray(c_end, dtype=jnp.int32)


def _nsa_dense(q, q_pos, gates, kc, vc, c_end, ks, vs, s_pos_len, kw, vw, w_pos):
    B, Tq = q.shape[:2]
    qg = q.reshape(B, Tq, KV_HEADS, Q_PER_KV, HEAD_DIM)
    s_c = jnp.einsum('btgrd,bcgd->btgrc', qg, kc) * ATT_SCALE
    m_c = (c_end[None, :] <= q_pos[:, None])[None, :, None, None, :]
    p_c = _masked_softmax(s_c, m_c)
    o_c = jnp.einsum('btgrc,bcgd->btgrd', p_c, vc)
    n_sel = ks.shape[1] // SEL_BLK
    imp = jnp.einsum('btgrc,cs->btgs', p_c, _block_overlap(kc.shape[1], n_sel), precision=HI)
    j = jnp.arange(n_sel)
    cur = q_pos // SEL_BLK
    valid = j[None, :] * SEL_BLK <= q_pos[:, None]
    forced = (j[None, :] == 0) | (j[None, :] == cur[:, None]) | (j[None, :] == cur[:, None] - 1)
    score = jnp.where(valid[None, :, None, :], imp + FORCE_BONUS * forced[None, :, None, :].astype(jnp.float32), NEG)
    sj = score[..., :, None]
    si = score[..., None, :]
    ahead = (si > sj) | ((si == sj) & (j[None, :] < j[:, None]))
    rank = jnp.sum(ahead, axis=-1)
    sel = (rank < min(N_SEL_TOP, n_sel)) & valid[None, :, None, :]
    k_pos = jnp.arange(n_sel * SEL_BLK)
    m_s = jnp.repeat(sel, SEL_BLK, axis=-1) & (k_pos[None, :] <= q_pos[:, None])[None, :, None, :]
    s_s = jnp.einsum('btgrd,bkgd->btgrk', qg, ks) * ATT_SCALE
    p_s = _masked_softmax(s_s, m_s[:, :, :, None, :])
    o_s = jnp.einsum('btgrk,bkgd->btgrd', p_s, vs)
    s_w = jnp.einsum('btgrd,bkgd->btgrk', qg, kw) * ATT_SCALE
    delta = q_pos[:, None] - w_pos[None, :]
    m_w = ((delta >= 0) & (delta < WINDOW) & (w_pos[None, :] >= 0))[None, :, None, None, :]
    o_w = jnp.einsum('btgrk,bkgd->btgrd', _masked_softmax(s_w, m_w), vw)
    g = gates.reshape(B, Tq, KV_HEADS, Q_PER_KV, 3)
    o = g[..., 0:1] * o_c + g[..., 1:2] * o_s + g[..., 2:3] * o_w
    return o.reshape(B, Tq, ATT_W)


def _conv_module(cin, buf, conv_w, conv_b, ln_g, ln_b):
    a, gt = jnp.split(cin, 2, axis=-1)
    u = a * jax.nn.sigmoid(gt)
    seq = jnp.concatenate([buf, u], axis=1)
    T = u.shape[1]
    y = conv_b
    for j in range(CONV_W):
        y = y + seq[:, j:j + T] * conv_w[j]
    y = _layer_norm(y, ln_g, ln_b)
    return jax.nn.silu(y), seq[:, -(CONV_W - 1):]


def _split_kv(h, lo):
    k = h[..., lo:lo + KV_COLS].reshape(h.shape[:-1] + (KV_HEADS, HEAD_DIM))
    v = h[..., lo + KV_COLS:lo + 2 * KV_COLS].reshape(h.shape[:-1] + (KV_HEADS, HEAD_DIM))
    return k, v


def kernel(x_prompt, x_sample, cache_cmp_kv, cache_sel_kv, state_win_kv, state_conv, page_table,
           w_in, cmp_pe, cmp_w1, cmp_b1, cmp_w2, cmp_b2, conv_w, conv_b, conv_ln_g, conv_ln_b,
           w_out, ln1_g, ln1_b, router_w, router_b, exp_w_up, exp_b_up, exp_w_down, exp_b_down,
           ln2_g, ln2_b):
    B, T, _ = x_prompt.shape
    DB, DS, _ = x_sample.shape
    n_pages = page_table.shape[1]
    past = n_pages * PAGE_SIZE
    NP, NS = B * T, DB * DS
    pos_p = jnp.arange(T, dtype=jnp.int32)
    pos_s = past + jnp.arange(DS, dtype=jnp.int32)
    x = jnp.concatenate([x_prompt.reshape(NP, D_MODEL), x_sample.reshape(NS, D_MODEL)], axis=0)
    outs = [[] for _ in range(8)]
    for l in range(DEPTH):
        xb = x.astype(jnp.bfloat16)
        w_main = jnp.concatenate([w_in[l][:, :QKV_COLS], w_in[l][:, QKV_COLS + GATE_COLS:]], axis=1)
        w_gate = jnp.pad(w_in[l][:, QKV_COLS:QKV_COLS + GATE_COLS], ((0, 0), (0, LANES - GATE_COLS)))
        h = _matmul(xb, w_main, tm=768, tn=512)
        gates_all = jax.nn.sigmoid(_matmul(xb, w_gate, tm=768, tn=LANES)[:, :GATE_COLS])

        att_parts, conv_parts = [], []
        for grp in range(2):
            if grp == 0:
                Bg, Tg, pos, sl = B, T, pos_p, slice(0, NP)
            else:
                Bg, Tg, pos, sl = DB, DS, pos_s, slice(NP, NP + NS)
            hg = h[sl].reshape(Bg, Tg, -1)
            gates = gates_all[sl].reshape(Bg, Tg, N_HEADS, 3)
            q = _rope(hg[..., :ATT_W].reshape(Bg, Tg, N_HEADS, HEAD_DIM), pos)
            kc_new, vc_new = _split_kv(hg, ATT_W)
            ks_new, vs_new = _split_kv(hg, ATT_W + 2 * KV_COLS)
            kw_new, vw_new = _split_kv(hg, ATT_W + 4 * KV_COLS)
            kvc = jnp.stack([_rope(kc_new, pos), vc_new], axis=2)
            kvs = jnp.stack([_rope(ks_new, pos), vs_new], axis=2)
            kvw = jnp.stack([_rope(kw_new, pos), vw_new], axis=2)
            cin = hg[..., QKV_COLS:]
            if grp == 0:
                kc, vc, c_end = _compress(kvc, cmp_pe[l], cmp_w1[l], cmp_b1[l], cmp_w2[l], cmp_b2[l])
                att = _nsa_dense(q, pos, gates, kc, vc, c_end, kvs[:, :, 0], kvs[:, :, 1], T,
                                 kvw[:, :, 0], kvw[:, :, 1], pos)
                conv_out, cbuf = _conv_module(cin, jnp.zeros((Bg, CONV_W - 1, C_CONV), cin.dtype),
                                              conv_w[l], conv_b[l], conv_ln_g[l], conv_ln_b[l])
                win = kvw[:, -WINDOW:]
                base = 0
            else:
                cmp_rows = jnp.concatenate(
                    [cache_cmp_kv[l][page_table].reshape(DB, past, 2, KV_HEADS, HEAD_DIM), kvc], axis=1)
                kc, vc, c_end = _compress(cmp_rows, cmp_pe[l], cmp_w1[l], cmp_b1[l], cmp_w2[l], cmp_b2[l])
                sel_rows = jnp.concatenate(
                    [cache_sel_kv[l][page_table].reshape(DB, past, 2, KV_HEADS, HEAD_DIM), kvs], axis=1)
                pad = (-sel_rows.shape[1]) % SEL_BLK
                sel_rows = jnp.pad(sel_rows, ((0, 0), (0, pad), (0, 0), (0, 0), (0, 0)))
                kw = jnp.concatenate([state_win_kv[l], kvw], axis=1)
                w_pos = past - WINDOW + jnp.arange(WINDOW + DS, dtype=jnp.int32)
                att = _nsa_dense(q, pos, gates, kc, vc, c_end, sel_rows[:, :, 0], sel_rows[:, :, 1], past + DS,
                                 kw[:, :, 0], kw[:, :, 1], w_pos)
                conv_out, cbuf = _conv_module(cin, state_conv[l], conv_w[l], conv_b[l], conv_ln_g[l], conv_ln_b[l])
                win = kw[:, -WINDOW:]
                base = 4
            att_parts.append(att.reshape(Bg * Tg, ATT_W))
            conv_parts.append(conv_out.reshape(Bg * Tg, C_CONV))
            outs[base + 0].append(kvc)
            outs[base + 1].append(kvs)
            outs[base + 2].append(win)
            outs[base + 3].append(cbuf)

        mixed = jnp.concatenate([jnp.concatenate(att_parts, axis=0), jnp.concatenate(conv_parts, axis=0)], axis=1)
        mix = _matmul(mixed.astype(jnp.bfloat16), w_out[l], tm=768, tn=512)
        x = _layer_norm(DN_ALPHA * x + mix, ln1_g[l], ln1_b[l])
        y = _moe(x, router_w[l], router_b[l], exp_w_up[l], exp_b_up[l], exp_w_down[l], exp_b_down[l])
        x = _layer_norm(DN_ALPHA * x + y, ln2_g[l], ln2_b[l])

    return (x[:NP].reshape(B, T, D_MODEL), x[NP:].reshape(DB, DS, D_MODEL)) + tuple(jnp.stack(o) for o in outs)
```

```python
import functools

import jax
import jax.numpy as jnp
import numpy as np
from jax import lax
from jax.experimental import pallas as pl
from jax.experimental.pallas import tpu as pltpu

D_MODEL = 2048
DEPTH = 2
PAGE_SIZE = 128
N_HEADS = 8
HEAD_DIM = 128
KV_HEADS = 2
Q_PER_KV = N_HEADS // KV_HEADS
ATT_W = N_HEADS * HEAD_DIM
KV_COLS = KV_HEADS * HEAD_DIM
KV_UNITS = 2 * KV_HEADS
CMP_BLK = 32
CMP_STRIDE = 16
CMP_R = CMP_BLK // CMP_STRIDE
SEL_BLK = 64
N_SEL_TOP = 16
WINDOW = 512
FORCE_BONUS = 1000.0
NEG = -1e30
BELOW_NEG = -3e38
ROPE_THETA = 10000.0
ATT_SCALE = HEAD_DIM ** -0.5
C_CONV = D_MODEL - ATT_W
CONV_W = 31
QKV_COLS = ATT_W + 6 * KV_COLS
GATE_COLS = 3 * N_HEADS
N_EXPERTS = 32
TOP_K = 4
D_FF = 2048
SWIGLU_LIMIT = 7.0
SWIGLU_ALPHA = 1.702
LN_EPS = 1e-5
DN_ALPHA = (2 * DEPTH) ** 0.25

LANES = 128
VMEM_LIMIT = 56 * 1024 * 1024
MOE_TM = 256
ROW_TM = 768
NSA_TQ = 128
NSA_TK = 256
PAGES_PER_STEP = 8
HI = lax.Precision.HIGHEST
F32 = jnp.float32
BF16 = jnp.bfloat16


def _params(n_axes):
    return pltpu.CompilerParams(dimension_semantics=("arbitrary",) * n_axes, vmem_limit_bytes=VMEM_LIMIT)


def _dot(a, b):
    return jnp.dot(a, b, preferred_element_type=F32)


def _dot_nt(a, b):
    return lax.dot_general(a, b, (((1,), (1,)), ((), ())), preferred_element_type=F32)


def _split_dot(p, w):
    hi = p.astype(BF16)
    r1 = p - hi.astype(F32)
    mid = r1.astype(BF16)
    lo = (r1 - mid.astype(F32)).astype(BF16)
    return _dot(hi, w) + _dot(mid, w) + _dot(lo, w)


def _mm_kernel(a_ref, w_ref, o_ref, wbf_ref, *, sigmoid):
    @pl.when(pl.program_id(1) == 0)
    def _():
        wbf_ref[...] = w_ref[...].astype(BF16)

    h = _dot(a_ref[...], wbf_ref[...])
    o_ref[...] = (jax.nn.sigmoid(h) if sigmoid else h).astype(o_ref.dtype)


def _matmul(a, w, tm, tn, sigmoid=False, out_dtype=F32):
    M, K = a.shape
    N = w.shape[1]
    assert M % tm == 0 and N % tn == 0
    return pl.pallas_call(
        functools.partial(_mm_kernel, sigmoid=sigmoid),
        grid=(N // tn, M // tm),
        in_specs=[pl.BlockSpec((tm, K), lambda j, i: (i, 0)),
                  pl.BlockSpec((K, tn), lambda j, i: (0, j))],
        out_specs=pl.BlockSpec((tm, tn), lambda j, i: (i, j)),
        out_shape=jax.ShapeDtypeStruct((M, N), out_dtype),
        scratch_shapes=[pltpu.VMEM((K, tn), BF16)],
        compiler_params=_params(2),
        name="dense_matmul",
    )(a, w)


def _qkv_kernel(a_ref, w_ref, cos_ref, sin_ref, o32_ref, o16_ref, wbf_ref):
    j = pl.program_id(0)

    @pl.when(pl.program_id(1) == 0)
    def _():
        wbf_ref[...] = w_ref[...].astype(BF16)

    h = _dot(a_ref[...], wbf_ref[...])
    cos = cos_ref[...]
    sin = sin_ref[...]
    q_tile = j < ATT_W // (4 * HEAD_DIM)
    for u in range(4):
        x = h[:, u * HEAD_DIM:(u + 1) * HEAD_DIM]
        y = x * cos + pltpu.roll(x, HEAD_DIM // 2, 1) * sin
        if u >= KV_HEADS:
            y = jnp.where(q_tile, y, x)
        o32_ref[:, u * HEAD_DIM:(u + 1) * HEAD_DIM] = y
        o16_ref[:, u * HEAD_DIM:(u + 1) * HEAD_DIM] = y.astype(BF16)


def _qkv_proj(a, w, cos_t, sin_t):
    M, K = a.shape
    N = w.shape[1]
    tn = 4 * HEAD_DIM
    return pl.pallas_call(
        _qkv_kernel,
        grid=(N // tn, M // ROW_TM),
        in_specs=[pl.BlockSpec((ROW_TM, K), lambda j, i: (i, 0)),
                  pl.BlockSpec((K, tn), lambda j, i: (0, j)),
                  pl.BlockSpec((ROW_TM, HEAD_DIM), lambda j, i: (i, 0)),
                  pl.BlockSpec((ROW_TM, HEAD_DIM), lambda j, i: (i, 0))],
        out_specs=[pl.BlockSpec((ROW_TM, tn), lambda j, i: (i, j)),
                   pl.BlockSpec((ROW_TM, tn), lambda j, i: (i, j))],
        out_shape=[jax.ShapeDtypeStruct((M, N), F32), jax.ShapeDtypeStruct((M, N), BF16)],
        scratch_shapes=[pltpu.VMEM((K, tn), BF16)],
        compiler_params=_params(2),
        name="qkv_proj",
    )(a, w, cos_t, sin_t)


def _masked_softmax(s, mask):
    s = jnp.where(mask, s, NEG)
    m = jnp.max(s, axis=-1, keepdims=True)
    e = jnp.where(mask, jnp.exp(s - m), 0.0)
    return e / jnp.maximum(jnp.sum(e, axis=-1, keepdims=True), 1e-30)


def _online_update(s, mask, v, m, l, acc):
    s = jnp.where(mask, s, NEG)
    m_new = jnp.maximum(m, jnp.max(s, axis=-1, keepdims=True))
    alpha = jnp.exp(m - m_new)
    p = jnp.where(mask, jnp.exp(s - m_new), 0.0)
    l = alpha * l + jnp.sum(p, axis=-1, keepdims=True)
    acc = alpha * acc + _dot(p.astype(BF16), v)
    return m_new, l, acc


def _online_init(rows):
    return (jnp.full((rows, 1), NEG, F32), jnp.zeros((rows, 1), F32), jnp.zeros((rows, HEAD_DIM), F32))


def _online_finish(l, acc):
    return acc / jnp.maximum(l, 1e-30)


def _select_blocks(imp, tpos, n_sel):
    L = imp.shape[1]
    jidx = lax.broadcasted_iota(jnp.int32, (1, L), 1)
    jf = jidx.astype(F32)
    cur = _div(tpos, SEL_BLK)
    valid = (jidx * SEL_BLK <= tpos) & (jidx < n_sel)
    forced = (jidx == 0) | (jidx == cur) | (jidx == cur - 1)
    score = jnp.where(valid, imp + jnp.where(forced, FORCE_BONUS, 0.0), NEG)
    score = jnp.where(jidx < n_sel, score, BELOW_NEG)
    picked = jnp.zeros(imp.shape, F32)
    for _ in range(min(N_SEL_TOP, n_sel)):
        m = jnp.max(score, axis=-1, keepdims=True)
        first = jnp.min(jnp.where(score == m, jf, float(L)), axis=-1, keepdims=True)
        pick = jf == first
        picked = jnp.where(pick, 1.0, picked)
        score = jnp.where(pick, BELOW_NEG, score)
    return jnp.where(valid, picked, 0.0)


def _overlap_matrix(n_cmp, n_sel, rows, cols):
    i = np.arange(rows)[:, None]
    j = np.arange(cols)[None, :]
    ov = (i * CMP_STRIDE < (j + 1) * SEL_BLK) & (i * CMP_STRIDE + CMP_BLK > j * SEL_BLK) & (i < n_cmp) & (j < n_sel)
    return jnp.asarray(ov.astype(np.float32), dtype=BF16)


def _compress_body(step, n_steps, row_refs, pe_ref, w1_ref, b1_ref, w2_ref, b2_ref, o_ref, x_ref, n_ch):
    cpr = row_refs[0].shape[0] // (KV_UNITS * CMP_STRIDE)
    cps = cpr * len(row_refs)
    for k, ref in enumerate(row_refs):
        base = pl.multiple_of(step * cps + k * cpr, 8)
        for u in range(KV_UNITS):
            for s in range(CMP_STRIDE):
                col = (u * CMP_STRIDE + s) * HEAD_DIM
                x_ref[pl.ds(base, cpr), col:col + HEAD_DIM] = ref[pl.ds(KV_UNITS * s + u, cpr, stride=KV_UNITS * CMP_STRIDE), :]

    @pl.when(step == n_steps - 1)
    def _():
        x_ref[n_ch:n_ch + 16, :] = pe_ref[...]
        width = CMP_STRIDE * HEAD_DIM
        ridx = lax.broadcasted_iota(jnp.int32, (n_ch, 1), 0)
        for u in range(KV_UNITS):
            kk = u // KV_HEADS
            acc = _dot(x_ref[:, u * width:(u + 1) * width].astype(BF16), w1_ref[kk])
            pe_term = acc[n_ch:n_ch + 1, :HEAD_DIM] + acc[n_ch + 1:n_ch + 2, HEAD_DIM:]
            z = acc[:n_ch, :HEAD_DIM] + pltpu.roll(acc[:n_ch, HEAD_DIM:], n_ch - 1, 0) + (b1_ref[kk] + pe_term)
            c = _dot(jax.nn.gelu(z).astype(BF16), w2_ref[kk]) + b2_ref[kk]
            c = jnp.where(ridx < n_ch - (CMP_R - 1), c, 0.0)
            o_ref[:, u * HEAD_DIM:(u + 1) * HEAD_DIM] = c.astype(o_ref.dtype)


def _compress_prompt_kernel(rows_ref, pe_ref, w1_ref, b1_ref, w2_ref, b2_ref, o_ref, x_ref, *, n_ch):
    _compress_body(pl.program_id(1), pl.num_programs(1), [rows_ref], pe_ref, w1_ref, b1_ref, w2_ref, b2_ref,
                   o_ref, x_ref, n_ch)


def _compress_paged_kernel(pt_ref, *refs, n_ch):
    del pt_ref
    pages = list(refs[:PAGES_PER_STEP])
    pe_ref, w1_ref, b1_ref, w2_ref, b2_ref, o_ref, x_ref = refs[PAGES_PER_STEP:]
    _compress_body(pl.program_id(1), pl.num_programs(1), pages, pe_ref, w1_ref, b1_ref, w2_ref, b2_ref,
                   o_ref, x_ref, n_ch)


def _compress_weights(pe, w1, b1, w2, b2):
    hid = w1.shape[-1]
    w1r = w1.reshape(2, CMP_R, CMP_STRIDE, HEAD_DIM, hid)
    w1cat = w1r.transpose(0, 2, 3, 1, 4).reshape(2, CMP_STRIDE * HEAD_DIM, CMP_R * hid).astype(BF16)
    per = pe.reshape(2, CMP_R, CMP_STRIDE * HEAD_DIM)
    per = jnp.broadcast_to(per[:, None], (2, KV_HEADS, CMP_R, CMP_STRIDE * HEAD_DIM))
    per = per.transpose(2, 0, 1, 3).reshape(CMP_R, KV_UNITS * CMP_STRIDE * HEAD_DIM)
    pe_rows = jnp.concatenate([per, jnp.zeros((16 - CMP_R, per.shape[1]), F32)], axis=0)
    return pe_rows, w1cat, b1.reshape(2, 1, hid), w2.astype(BF16), b2.reshape(2, 1, HEAD_DIM)


def _compress_prompt(rows, cw):
    B, R4, _ = rows.shape
    n_ch = R4 // (KV_UNITS * CMP_STRIDE)
    n_steps = 2
    pe_rows, w1cat, b1, w2, b2 = cw
    xw = KV_UNITS * CMP_STRIDE * HEAD_DIM
    return pl.pallas_call(
        functools.partial(_compress_prompt_kernel, n_ch=n_ch),
        grid=(B, n_steps),
        in_specs=[pl.BlockSpec((None, R4 // n_steps, HEAD_DIM), lambda b, s: (b, s, 0)),
                  pl.BlockSpec(pe_rows.shape, lambda b, s: (0, 0)),
                  pl.BlockSpec(w1cat.shape, lambda b, s: (0, 0, 0)),
                  pl.BlockSpec(b1.shape, lambda b, s: (0, 0, 0)),
                  pl.BlockSpec(w2.shape, lambda b, s: (0, 0, 0)),
                  pl.BlockSpec(b2.shape, lambda b, s: (0, 0, 0))],
        out_specs=pl.BlockSpec((None, n_ch, KV_UNITS * HEAD_DIM), lambda b, s: (b, 0, 0)),
        out_shape=jax.ShapeDtypeStruct((B, n_ch, KV_UNITS * HEAD_DIM), BF16),
        scratch_shapes=[pltpu.VMEM((n_ch + 16, xw), F32)],
        compiler_params=_params(2),
        name="compress_prompt",
    )(rows, pe_rows, w1cat, b1, w2, b2)


def _compress_paged(cache, layer, page_table, cw):
    DB, n_pages = page_table.shape
    assert n_pages % PAGES_PER_STEP == 0
    n_steps = n_pages // PAGES_PER_STEP
    n_ch = n_pages * PAGE_SIZE // CMP_STRIDE
    pe_rows, w1cat, b1, w2, b2 = cw
    xw = KV_UNITS * CMP_STRIDE * HEAD_DIM

    def page_spec(k):
        return pl.BlockSpec((None, None, PAGE_SIZE * KV_UNITS, HEAD_DIM),
                            lambda b, s, pt: (layer, pt[b * n_pages + s * PAGES_PER_STEP + k], 0, 0))

    grid_spec = pltpu.PrefetchScalarGridSpec(
        num_scalar_prefetch=1,
        grid=(DB, n_steps),
        in_specs=[page_spec(k) for k in range(PAGES_PER_STEP)] + [
            pl.BlockSpec(pe_rows.shape, lambda b, s, pt: (0, 0)),
            pl.BlockSpec(w1cat.shape, lambda b, s, pt: (0, 0, 0)),
            pl.BlockSpec(b1.shape, lambda b, s, pt: (0, 0, 0)),
            pl.BlockSpec(w2.shape, lambda b, s, pt: (0, 0, 0)),
            pl.BlockSpec(b2.shape, lambda b, s, pt: (0, 0, 0))],
        out_specs=pl.BlockSpec((None, n_ch, KV_UNITS * HEAD_DIM), lambda b, s, pt: (b, 0, 0)),
        scratch_shapes=[pltpu.VMEM((n_ch + 16, xw), F32)],
    )
    return pl.pallas_call(
        functools.partial(_compress_paged_kernel, n_ch=n_ch),
        grid_spec=grid_spec,
        out_shape=jax.ShapeDtypeStruct((DB, n_ch, KV_UNITS * HEAD_DIM), BF16),
        compiler_params=_params(2),
        name="compress_paged",
    )(page_table.reshape(-1), *([cache] * PAGES_PER_STEP), pe_rows, w1cat, b1, w2, b2)


def _nsa_prompt_kernel(q_ref, ks_ref, kw_ref, kcv_ref, gate_ref, ov_ref, e_ref, o_ref, *, n_cmp, n_sel):
    TQ, TK = NSA_TQ, NSA_TK
    R = Q_PER_KV * TQ
    t0 = pl.program_id(1) * TQ
    tpos = t0 + lax.broadcasted_iota(jnp.int32, (TQ, 1), 0)
    tpos4 = jnp.concatenate([tpos] * Q_PER_KV, axis=0)
    NC = kcv_ref.shape[0]
    cidx = lax.broadcasted_iota(jnp.int32, (1, NC), 1)
    m_c = (cidx * CMP_STRIDE + (CMP_BLK - 1) <= tpos4) & (cidx < n_cmp)
    klane = lax.broadcasted_iota(jnp.int32, (1, TK), 1)
    n_sel_tiles = (t0 + TQ + TK - 1) // TK
    w_lo = jnp.maximum(t0 - WINDOW, 0) // TK
    w_hi = (t0 + TQ - 1) // TK + 1

    for g in range(KV_HEADS):
        q4 = jnp.concatenate([q_ref[:, (Q_PER_KV * g + r) * HEAD_DIM:(Q_PER_KV * g + r + 1) * HEAD_DIM]
                              for r in range(Q_PER_KV)], axis=0)
        kcol = slice(g * HEAD_DIM, (g + 1) * HEAD_DIM)
        vcol = slice((KV_HEADS + g) * HEAD_DIM, (KV_HEADS + g + 1) * HEAD_DIM)

        p_c = _masked_softmax(_dot_nt(q4, kcv_ref[:, kcol]) * ATT_SCALE, m_c)
        o_c = _dot(p_c.astype(BF16), kcv_ref[:, vcol])
        imp4 = _split_dot(p_c, ov_ref[...])
        imp = imp4[0:TQ]
        for r in range(1, Q_PER_KV):
            imp = imp + imp4[r * TQ:(r + 1) * TQ]
        sel = _select_blocks(imp, tpos, n_sel)
        sel4 = jnp.concatenate([sel] * Q_PER_KV, axis=0).astype(BF16)

        def sel_step(kb, carry):
            k0 = pl.multiple_of(kb * TK, TK)
            s = _dot_nt(q4, ks_ref[pl.ds(k0, TK), kcol]) * ATT_SCALE
            in_sel = _dot(sel4, e_ref[kb]) > 0.5
            mask = in_sel & (k0 + klane <= tpos4)
            return _online_update(s, mask, ks_ref[pl.ds(k0, TK), vcol], *carry)

        _, l_s, acc_s = lax.fori_loop(0, n_sel_tiles, sel_step, _online_init(R))
        o_s = _online_finish(l_s, acc_s)

        def win_step(kb, carry):
            k0 = pl.multiple_of(kb * TK, TK)
            s = _dot_nt(q4, kw_ref[pl.ds(k0, TK), kcol]) * ATT_SCALE
            delta = tpos4 - (k0 + klane)
            mask = (delta >= 0) & (delta < WINDOW)
            return _online_update(s, mask, kw_ref[pl.ds(k0, TK), vcol], *carry)

        _, l_w, acc_w = lax.fori_loop(w_lo, w_hi, win_step, _online_init(R))
        o_w = _online_finish(l_w, acc_w)

        for r in range(Q_PER_KV):
            h = Q_PER_KV * g + r
            rows = slice(r * TQ, (r + 1) * TQ)
            o = (gate_ref[:, 3 * h:3 * h + 1] * o_c[rows] + gate_ref[:, 3 * h + 1:3 * h + 2] * o_s[rows]
                 + gate_ref[:, 3 * h + 2:3 * h + 3] * o_w[rows])
            o_ref[:, h * HEAD_DIM:(h + 1) * HEAD_DIM] = o.astype(o_ref.dtype)


def _nsa_prompt(h16, kcv, gates, B, T):
    TQ, TK = NSA_TQ, NSA_TK
    nq = T // TQ
    NC = kcv.shape[1]
    n_cmp, n_sel = NC - (CMP_R - 1), T // SEL_BLK
    ov = _overlap_matrix(n_cmp, n_sel, NC, LANES)
    kidx = np.arange(T) // SEL_BLK
    e = (np.arange(LANES)[None, :, None] == kidx.reshape(T // TK, 1, TK)).astype(np.float32)
    e = jnp.asarray(e, dtype=BF16)
    kv_w = 2 * KV_COLS
    return pl.pallas_call(
        functools.partial(_nsa_prompt_kernel, n_cmp=n_cmp, n_sel=n_sel),
        grid=(B, nq),
        in_specs=[pl.BlockSpec((TQ, ATT_W), lambda b, i: (b * nq + i, 0)),
                  pl.BlockSpec((T, kv_w), lambda b, i: (b, (ATT_W + kv_w) // kv_w)),
                  pl.BlockSpec((T, kv_w), lambda b, i: (b, (ATT_W + 2 * kv_w) // kv_w)),
                  pl.BlockSpec((None, NC, KV_UNITS * HEAD_DIM), lambda b, i: (b, 0, 0)),
                  pl.BlockSpec((TQ, LANES), lambda b, i: (b * nq + i, 0)),
                  pl.BlockSpec(ov.shape, lambda b, i: (0, 0)),
                  pl.BlockSpec(e.shape, lambda b, i: (0, 0, 0))],
        out_specs=pl.BlockSpec((TQ, ATT_W), lambda b, i: (b * nq + i, 0)),
        out_shape=jax.ShapeDtypeStruct((B * T, ATT_W), BF16),
        compiler_params=_params(2),
        name="nsa_prompt",
    )(h16, h16, h16, kcv, gates, ov, e)


def _unit_rows(ref, u, n):
    return ref[pl.ds(u, n, stride=KV_UNITS), :]


def _both_heads(ref, kv, n, pad_to=None):
    parts = []
    for g in range(KV_HEADS):
        parts.append(_unit_rows(ref, kv * KV_HEADS + g, n))
        if pad_to is not None and pad_to > n:
            parts.append(jnp.zeros((pad_to - n, HEAD_DIM), F32))
    return jnp.concatenate(parts, axis=0).astype(BF16)


def _log2(n):
    assert n > 0 and n & (n - 1) == 0, n
    return n.bit_length() - 1


def _div(x, n):
    return jnp.right_shift(x, _log2(n))


def _mod(x, n):
    _log2(n)
    return jnp.bitwise_and(x, n - 1)


def _nsa_sample_kernel(pt_ref, *refs, past, n_cmp, n_sel, DS):
    del pt_ref
    pages = refs[:PAGES_PER_STEP]
    (q_ref, kcv_ref, snew_ref, wst_ref, wnew_ref, gate_ref, ov_ref, o_ref,
     sel_ref, oc_ref, m_ref, l_ref, acc_ref) = refs[PAGES_PER_STEP:]
    step = pl.program_id(1)
    R = N_HEADS * DS
    RG = Q_PER_KV * DS
    ridx = lax.broadcasted_iota(jnp.int32, (R, 1), 0)
    tpos = past + _mod(ridx, DS)
    row_g = _div(ridx, RG)
    q = jnp.concatenate([q_ref[:, h * HEAD_DIM:(h + 1) * HEAD_DIM] for h in range(N_HEADS)], axis=0).astype(BF16)

    def same_head(n_lanes, per_head):
        lane = lax.broadcasted_iota(jnp.int32, (1, n_lanes), 1)
        return (_div(lane, per_head) == row_g), _mod(lane, per_head)

    @pl.when(step == 0)
    def _():
        NC = kcv_ref.shape[0]
        kc = jnp.concatenate([kcv_ref[:, g * HEAD_DIM:(g + 1) * HEAD_DIM] for g in range(KV_HEADS)], axis=0)
        vc = jnp.concatenate([kcv_ref[:, (KV_HEADS + g) * HEAD_DIM:(KV_HEADS + g + 1) * HEAD_DIM]
                              for g in range(KV_HEADS)], axis=0)
        same, cidx = same_head(KV_HEADS * NC, NC)
        m_c = same & (cidx * CMP_STRIDE + (CMP_BLK - 1) <= tpos) & (cidx < n_cmp)
        p_c = _masked_softmax(_dot_nt(q, kc) * ATT_SCALE, m_c)
        oc_ref[...] = _dot(p_c.astype(BF16), vc)
        ov2 = jnp.concatenate([ov_ref[...]] * KV_HEADS, axis=0)
        imp_all = _split_dot(p_c, ov2)
        for g in range(KV_HEADS):
            imp = imp_all[g * RG:g * RG + DS]
            for r in range(1, Q_PER_KV):
                imp = imp + imp_all[g * RG + r * DS:g * RG + (r + 1) * DS]
            sel = _select_blocks(imp, tpos[0:DS], n_sel)
            sel_ref[g * RG:(g + 1) * RG, :] = jnp.concatenate([sel] * Q_PER_KV, axis=0)
        m0, l0, a0 = _online_init(R)
        m_ref[...] = m0
        l_ref[...] = l0
        acc_ref[...] = a0

    sel_b = sel_ref[...].astype(BF16)
    L = sel_ref.shape[1]
    jrow = lax.broadcasted_iota(jnp.int32, (L, 1), 0)
    lane = lax.broadcasted_iota(jnp.int32, (1, PAGES_PER_STEP * KV_HEADS * PAGE_SIZE), 1)
    kpos = (step * PAGES_PER_STEP + _div(lane, KV_HEADS * PAGE_SIZE)) * PAGE_SIZE + _mod(lane, PAGE_SIZE)
    same = _mod(_div(lane, PAGE_SIZE), KV_HEADS) == row_g
    expand = jnp.where(jrow == _div(kpos, SEL_BLK), 1.0, 0.0).astype(BF16)
    mask = same & (_dot(sel_b, expand) > 0.5) & (kpos <= tpos)
    k_all = jnp.concatenate([_both_heads(pg, 0, PAGE_SIZE) for pg in pages], axis=0)
    v_all = jnp.concatenate([_both_heads(pg, 1, PAGE_SIZE) for pg in pages], axis=0)
    s = _dot_nt(q, k_all) * ATT_SCALE
    m_ref[...], l_ref[...], acc_ref[...] = _online_update(s, mask, v_all, m_ref[...], l_ref[...], acc_ref[...])

    @pl.when(step == pl.num_programs(1) - 1)
    def _():
        same_n, tn = same_head(KV_HEADS * LANES, LANES)
        same_n = same_n & (tn < DS)
        npos = past + tn
        expand = jnp.where(jrow == _div(npos, SEL_BLK), 1.0, 0.0).astype(BF16)
        mask = same_n & (_dot(sel_b, expand) > 0.5) & (npos <= tpos)
        s = _dot_nt(q, _both_heads(snew_ref, 0, DS, LANES)) * ATT_SCALE
        _, l_s, acc_s = _online_update(s, mask, _both_heads(snew_ref, 1, DS, LANES),
                                       m_ref[...], l_ref[...], acc_ref[...])
        o_s = _online_finish(l_s, acc_s)

        same_w, win = same_head(KV_HEADS * WINDOW, WINDOW)
        wpos = past - WINDOW + win
        delta = tpos - wpos
        mask = same_w & (delta >= 0) & (delta < WINDOW) & (wpos >= 0)
        s = _dot_nt(q, _both_heads(wst_ref, 0, WINDOW)) * ATT_SCALE
        carry_w = _online_update(s, mask, _both_heads(wst_ref, 1, WINDOW), *_online_init(R))
        delta = tpos - npos
        mask = same_n & (delta >= 0) & (delta < WINDOW)
        s = _dot_nt(q, _both_heads(wnew_ref, 0, DS, LANES)) * ATT_SCALE
        _, l_w, acc_w = _online_update(s, mask, _both_heads(wnew_ref, 1, DS, LANES), *carry_w)
        o_w = _online_finish(l_w, acc_w)

        o_c = oc_ref[...]
        for h in range(N_HEADS):
            rows = slice(h * DS, (h + 1) * DS)
            o = (gate_ref[:, 3 * h:3 * h + 1] * o_c[rows] + gate_ref[:, 3 * h + 1:3 * h + 2] * o_s[rows]
                 + gate_ref[:, 3 * h + 2:3 * h + 3] * o_w[rows])
            o_ref[:, h * HEAD_DIM:(h + 1) * HEAD_DIM] = o


def _nsa_sample(q, kcv, sel_new, win_state, win_new, gates, cache_sel, layer, page_table):
    DB, DS, _ = q.shape
    n_pages = page_table.shape[1]
    past = n_pages * PAGE_SIZE
    assert n_pages % PAGES_PER_STEP == 0 and DS % 8 == 0 and SEL_BLK % DS == 0
    n_steps = n_pages // PAGES_PER_STEP
    NC = kcv.shape[1]
    n_cmp = (past + DS) // CMP_STRIDE - (CMP_R - 1)
    assert NC == n_cmp + (CMP_R - 1)
    n_sel = -(-(past + DS) // SEL_BLK)
    L = -(-n_sel // LANES) * LANES
    ov = _overlap_matrix(n_cmp, n_sel, NC, L)
    R = N_HEADS * DS

    def page_spec(k):
        return pl.BlockSpec((None, None, PAGE_SIZE * KV_UNITS, HEAD_DIM),
                            lambda b, s, pt: (layer, pt[b * n_pages + s * PAGES_PER_STEP + k], 0, 0))

    per_b = lambda shape: pl.BlockSpec((None,) + shape, lambda b, s, pt: (b,) + (0,) * len(shape))
    grid_spec = pltpu.PrefetchScalarGridSpec(
        num_scalar_prefetch=1,
        grid=(DB, n_steps),
        in_specs=[page_spec(k) for k in range(PAGES_PER_STEP)] + [
            per_b((DS, ATT_W)),
            per_b((NC, KV_UNITS * HEAD_DIM)),
            per_b((DS * KV_UNITS, HEAD_DIM)),
            pl.BlockSpec((None, None, WINDOW * KV_UNITS, HEAD_DIM), lambda b, s, pt: (layer, b, 0, 0)),
            per_b((DS * KV_UNITS, HEAD_DIM)),
            per_b((DS, LANES)),
            pl.BlockSpec(ov.shape, lambda b, s, pt: (0, 0))],
        out_specs=per_b((DS, ATT_W)),
        scratch_shapes=[pltpu.VMEM((R, L), F32), pltpu.VMEM((R, HEAD_DIM), F32), pltpu.VMEM((R, 1), F32),
                        pltpu.VMEM((R, 1), F32), pltpu.VMEM((R, HEAD_DIM), F32)],
    )
    return pl.pallas_call(
        functools.partial(_nsa_sample_kernel, past=past, n_cmp=n_cmp, n_sel=n_sel, DS=DS),
        grid_spec=grid_spec,
        out_shape=jax.ShapeDtypeStruct((DB, DS, ATT_W), F32),
        compiler_params=_params(2),
        name="nsa_sample",
    )(page_table.reshape(-1), *([cache_sel] * PAGES_PER_STEP), q, kcv, sel_new, win_state, win_new, gates, ov)


def _moe_up_kernel(be_ref, bv_ref, x_ref, w_ref, b_ref, pick_ref, o_ref, wbf_ref):
    i = pl.program_id(1)
    prev_e = be_ref[jnp.maximum(i - 1, 0)]

    @pl.when((i == 0) | (be_ref[i] != prev_e))
    def _():
        wbf_ref[...] = w_ref[...].astype(BF16)

    @pl.when(bv_ref[i] != 0)
    def _():
        h = _dot(x_ref[...], wbf_ref[...]) + b_ref[...]
        g = jnp.minimum(h, SWIGLU_LIMIT)
        lin = jnp.clip(h, -SWIGLU_LIMIT, SWIGLU_LIMIT) + 1.0
        t = g * jax.nn.sigmoid(SWIGLU_ALPHA * g) * pltpu.roll(lin, h.shape[1] - 1, 1)
        o_ref[...] = _dot(t.astype(BF16), pick_ref[...]).astype(o_ref.dtype)

    @pl.when(bv_ref[i] == 0)
    def _():
        o_ref[...] = jnp.zeros_like(o_ref)


def _moe_up(block_e, block_valid, xs, w_up, b_up, tn):
    rows, D = xs.shape
    F = w_up.shape[2] // 2
    nb = rows // MOE_TM
    pick = jnp.asarray(np.arange(2 * tn)[:, None] == 2 * np.arange(tn)[None, :], dtype=BF16)
    grid_spec = pltpu.PrefetchScalarGridSpec(
        num_scalar_prefetch=2,
        grid=(F // tn, nb),
        in_specs=[pl.BlockSpec((MOE_TM, D), lambda j, i, be, bv: (i, 0)),
                  pl.BlockSpec((None, D, 2 * tn), lambda j, i, be, bv: (be[i], 0, j)),
                  pl.BlockSpec((None, 1, 2 * tn), lambda j, i, be, bv: (be[i], 0, j)),
                  pl.BlockSpec((2 * tn, tn), lambda j, i, be, bv: (0, 0))],
        out_specs=pl.BlockSpec((MOE_TM, tn), lambda j, i, be, bv: (i, j)),
        scratch_shapes=[pltpu.VMEM((D, 2 * tn), BF16)],
    )
    return pl.pallas_call(
        _moe_up_kernel,
        grid_spec=grid_spec,
        out_shape=jax.ShapeDtypeStruct((rows, F), BF16),
        compiler_params=_params(2),
        name="moe_up",
    )(block_e, block_valid, xs, w_up, b_up, pick)


def _moe_down_kernel(be_ref, bv_ref, a_ref, w_ref, b_ref, rw_ref, o_ref, wbf_ref):
    i = pl.program_id(1)
    prev_e = be_ref[jnp.maximum(i - 1, 0)]

    @pl.when((i == 0) | (be_ref[i] != prev_e))
    def _():
        wbf_ref[...] = w_ref[...].astype(BF16)

    @pl.when(bv_ref[i] != 0)
    def _():
        h = _dot(a_ref[...], wbf_ref[...]) + b_ref[...]
        o_ref[...] = h * rw_ref[...]

    @pl.when(bv_ref[i] == 0)
    def _():
        o_ref[...] = jnp.zeros_like(o_ref)


def _moe_down(block_e, block_valid, act, w_down, b_down, row_w, tn):
    rows, F = act.shape
    D = w_down.shape[2]
    nb = rows // MOE_TM
    grid_spec = pltpu.PrefetchScalarGridSpec(
        num_scalar_prefetch=2,
        grid=(D // tn, nb),
        in_specs=[pl.BlockSpec((MOE_TM, F), lambda j, i, be, bv: (i, 0)),
                  pl.BlockSpec((None, F, tn), lambda j, i, be, bv: (be[i], 0, j)),
                  pl.BlockSpec((None, 1, tn), lambda j, i, be, bv: (be[i], 0, j)),
                  pl.BlockSpec((MOE_TM, 1), lambda j, i, be, bv: (i, 0))],
        out_specs=pl.BlockSpec((MOE_TM, tn), lambda j, i, be, bv: (i, j)),
        scratch_shapes=[pltpu.VMEM((F, tn), BF16)],
    )
    return pl.pallas_call(
        _moe_down_kernel,
        grid_spec=grid_spec,
        out_shape=jax.ShapeDtypeStruct((rows, D), F32),
        compiler_params=_params(2),
        name="moe_down",
    )(block_e, block_valid, act, w_down, b_down, row_w)


def _moe(x, router_w, router_b, w_up, b_up, w_down, b_down):
    N, D = x.shape
    logits = jnp.dot(x, router_w, precision=HI) + router_b
    top_v, top_e = lax.top_k(logits, TOP_K)
    gate = jax.nn.softmax(top_v, axis=-1)
    M = N * TOP_K
    flat_e = top_e.reshape(M)
    order = jnp.argsort(flat_e)
    se = flat_e[order]
    counts = jnp.sum(flat_e[:, None] == jnp.arange(N_EXPERTS, dtype=flat_e.dtype)[None, :], axis=0).astype(jnp.int32)
    padded = (counts + MOE_TM - 1) // MOE_TM * MOE_TM
    start = jnp.cumsum(counts) - counts
    pend = jnp.cumsum(padded)
    pstart = pend - padded
    dest = pstart[se] + (jnp.arange(M, dtype=jnp.int32) - start[se])
    nb = -(-(M + N_EXPERTS * (MOE_TM - 1)) // MOE_TM)
    rows = nb * MOE_TM
    row_tok = jnp.full((rows,), N, jnp.int32).at[dest].set((order // TOP_K).astype(jnp.int32))
    row_w = jnp.zeros((rows,), F32).at[dest].set(gate.reshape(M)[order])
    slot = jnp.zeros((M,), jnp.int32).at[order].set(dest)
    blk0 = jnp.arange(nb, dtype=jnp.int32) * MOE_TM
    block_e = jnp.clip(jnp.searchsorted(pend, blk0, side='right'), 0, N_EXPERTS - 1).astype(jnp.int32)
    block_valid = (blk0 < pend[-1]).astype(jnp.int32)

    x_pad = jnp.concatenate([x.astype(BF16), jnp.zeros((1, D), BF16)], axis=0)
    xs = x_pad[row_tok]
    act = _moe_up(block_e, block_valid, xs, w_up, b_up.reshape(N_EXPERTS, 1, 2 * D_FF), tn=512)
    out = _moe_down(block_e, block_valid, act, w_down, b_down.reshape(N_EXPERTS, 1, D), row_w[:, None], tn=1024)
    return jnp.sum(out[slot.reshape(N, TOP_K)], axis=1)


def _layer_norm(x, g, b):
    mu = jnp.mean(x, axis=-1, keepdims=True)
    var = jnp.mean(jnp.square(x - mu), axis=-1, keepdims=True)
    return (x - mu) * lax.rsqrt(var + LN_EPS) * g + b


def _rope_tables(pos):
    inv = ROPE_THETA ** (-jnp.arange(0, HEAD_DIM, 2, dtype=F32) / HEAD_DIM)
    ang = pos.astype(F32)[:, None] * inv[None, :]
    cos, sin = jnp.cos(ang), jnp.sin(ang)
    return jnp.concatenate([cos, cos], axis=-1), jnp.concatenate([-sin, sin], axis=-1)


def _conv_module(cin, buf, conv_w, conv_b, ln_g, ln_b):
    a, gt = jnp.split(cin, 2, axis=-1)
    u = a * jax.nn.sigmoid(gt)
    seq = jnp.concatenate([buf, u], axis=1)
    T = u.shape[1]
    y = conv_b
    for j in range(CONV_W):
        y = y + seq[:, j:j + T] * conv_w[j]
    y = _layer_norm(y, ln_g, ln_b)
    return jax.nn.silu(y), seq[:, -(CONV_W - 1):]


def kernel(x_prompt, x_sample, cache_cmp_kv, cache_sel_kv, state_win_kv, state_conv, page_table,
           w_in, cmp_pe, cmp_w1, cmp_b1, cmp_w2, cmp_b2, conv_w, conv_b, conv_ln_g, conv_ln_b,
           w_out, ln1_g, ln1_b, router_w, router_b, exp_w_up, exp_b_up, exp_w_down, exp_b_down,
           ln2_g, ln2_b):
    B, T, _ = x_prompt.shape
    DB, DS, _ = x_sample.shape
    n_pages = page_table.shape[1]
    n_pool = cache_cmp_kv.shape[1]
    past = n_pages * PAGE_SIZE
    NP, NS = B * T, DB * DS
    pos = jnp.concatenate([jnp.tile(jnp.arange(T, dtype=jnp.int32), B),
                           jnp.tile(past + jnp.arange(DS, dtype=jnp.int32), DB)])
    cos_t, sin_t = _rope_tables(pos)
    cache_cmp = cache_cmp_kv.reshape(DEPTH, n_pool, PAGE_SIZE * KV_UNITS, HEAD_DIM)
    cache_sel = cache_sel_kv.reshape(DEPTH, n_pool, PAGE_SIZE * KV_UNITS, HEAD_DIM)
    win_state = state_win_kv.reshape(DEPTH, DB, WINDOW * KV_UNITS, HEAD_DIM)
    kv_shape = (2, KV_HEADS, HEAD_DIM)
    x = jnp.concatenate([x_prompt.reshape(NP, D_MODEL), x_sample.reshape(NS, D_MODEL)], axis=0)
    outs = [[] for _ in range(8)]
    for l in range(DEPTH):
        xb = x.astype(BF16)
        w_gate = jnp.pad(w_in[l][:, QKV_COLS:QKV_COLS + GATE_COLS], ((0, 0), (0, LANES - GATE_COLS)))
        h32, h16 = _qkv_proj(xb, w_in[l][:, :QKV_COLS], cos_t, sin_t)
        cin = _matmul(xb, w_in[l][:, QKV_COLS + GATE_COLS:], tm=ROW_TM, tn=512)
        gates = _matmul(xb, w_gate, tm=ROW_TM, tn=LANES, sigmoid=True)

        def kv_rows(lo, sl, n_seq):
            return h32[sl, lo:lo + 2 * KV_COLS].reshape(n_seq, -1, HEAD_DIM)

        cw = _compress_weights(cmp_pe[l], cmp_w1[l], cmp_b1[l], cmp_w2[l], cmp_b2[l])
        p_sl, s_sl = slice(0, NP), slice(NP, NP + NS)
        kvc_p, kvs_p, kvw_p = (kv_rows(ATT_W + i * 2 * KV_COLS, p_sl, B) for i in range(3))
        kvc_s, kvs_s, kvw_s = (kv_rows(ATT_W + i * 2 * KV_COLS, s_sl, DB) for i in range(3))

        kcv_p = _compress_prompt(kvc_p, cw)
        att_p = _nsa_prompt(h16, kcv_p, gates, B, T)
        conv_p, cbuf_p = _conv_module(cin[p_sl].reshape(B, T, -1), jnp.zeros((B, CONV_W - 1, C_CONV), F32),
                                      conv_w[l], conv_b[l], conv_ln_g[l], conv_ln_b[l])
        assert (past + DS) // CMP_STRIDE == past // CMP_STRIDE
        kcv_s = _compress_paged(cache_cmp, l, page_table, cw)
        att_s = _nsa_sample(h16[s_sl, :ATT_W].astype(F32).reshape(DB, DS, ATT_W), kcv_s, kvs_s, win_state, kvw_s,
                            gates[s_sl].reshape(DB, DS, LANES), cache_sel, l, page_table)
        conv_s, cbuf_s = _conv_module(cin[s_sl].reshape(DB, DS, -1), state_conv[l],
                                      conv_w[l], conv_b[l], conv_ln_g[l], conv_ln_b[l])

        outs[0].append(kvc_p.reshape((B, T) + kv_shape))
        outs[1].append(kvs_p.reshape((B, T) + kv_shape))
        outs[2].append(kvw_p.reshape((B, T) + kv_shape)[:, -WINDOW:])
        outs[3].append(cbuf_p)
        outs[4].append(kvc_s.reshape((DB, DS) + kv_shape))
        outs[5].append(kvs_s.reshape((DB, DS) + kv_shape))
        outs[6].append(jnp.concatenate([state_win_kv[l], kvw_s.reshape((DB, DS) + kv_shape)], axis=1)[:, -WINDOW:])
        outs[7].append(cbuf_s)

        att = jnp.concatenate([att_p, att_s.reshape(NS, ATT_W).astype(BF16)], axis=0)
        conv_out = jnp.concatenate([conv_p.reshape(NP, C_CONV), conv_s.reshape(NS, C_CONV)], axis=0).astype(BF16)
        mix = _matmul(jnp.concatenate([att, conv_out], axis=1), w_out[l], tm=ROW_TM, tn=512)
        x = _layer_norm(DN_ALPHA * x + mix, ln1_g[l], ln1_b[l])
        y = _moe(x, router_w[l], router_b[l], exp_w_up[l], exp_b_up[l], exp_w_down[l], exp_b_down[l])
        x = _layer_norm(DN_ALPHA * x + y, ln2_g[l], ln2_b[l])

    return (x[:NP].reshape(B, T, D_MODEL), x[NP:].reshape(DB, DS, D_MODEL)) + tuple(jnp.stack(o) for o in outs)
```

```python
import functools

import jax
import jax.numpy as jnp
import numpy as np
from jax import lax
from jax.experimental import pallas as pl
from jax.experimental.pallas import tpu as pltpu

D_MODEL = 2048
DEPTH = 2
PAGE_SIZE = 128
N_HEADS = 8
HEAD_DIM = 128
KV_HEADS = 2
Q_PER_KV = N_HEADS // KV_HEADS
ATT_W = N_HEADS * HEAD_DIM
KV_COLS = KV_HEADS * HEAD_DIM
KV_UNITS = 2 * KV_HEADS
CMP_BLK = 32
CMP_STRIDE = 16
CMP_R = CMP_BLK // CMP_STRIDE
SEL_BLK = 64
N_SEL_TOP = 16
WINDOW = 512
FORCE_BONUS = 1000.0
NEG = -1e30
BELOW_NEG = -3e38
ROPE_THETA = 10000.0
ATT_SCALE = HEAD_DIM ** -0.5
C_CONV = D_MODEL - ATT_W
CONV_W = 31
QKV_COLS = ATT_W + 6 * KV_COLS
GATE_COLS = 3 * N_HEADS
N_EXPERTS = 32
TOP_K = 4
D_FF = 2048
SWIGLU_LIMIT = 7.0
SWIGLU_ALPHA = 1.702
LN_EPS = 1e-5
DN_ALPHA = (2 * DEPTH) ** 0.25

LANES = 128
VMEM_LIMIT = 56 * 1024 * 1024
MOE_TM = 256
ROW_TM = 768
NSA_TQ = 128
NSA_TK = 256
PAGES_PER_STEP = 8
CONV_HALO = 32
HI = lax.Precision.HIGHEST
F32 = jnp.float32
BF16 = jnp.bfloat16


def _params(n_axes):
    return pltpu.CompilerParams(dimension_semantics=("arbitrary",) * n_axes, vmem_limit_bytes=VMEM_LIMIT)


def _dot(a, b):
    return jnp.dot(a, b, preferred_element_type=F32)


def _dot_nt(a, b):
    return lax.dot_general(a, b, (((1,), (1,)), ((), ())), preferred_element_type=F32)


def _split_dot(p, w):
    hi = p.astype(BF16)
    r1 = p - hi.astype(F32)
    mid = r1.astype(BF16)
    lo = (r1 - mid.astype(F32)).astype(BF16)
    return _dot(hi, w) + _dot(mid, w) + _dot(lo, w)


def _mm_kernel(a_ref, w_ref, o_ref, wbf_ref, *, sigmoid):
    @pl.when(pl.program_id(1) == 0)
    def _():
        wbf_ref[...] = w_ref[...].astype(BF16)

    h = _dot(a_ref[...], wbf_ref[...])
    o_ref[...] = (jax.nn.sigmoid(h) if sigmoid else h).astype(o_ref.dtype)


def _matmul(a, w, tm, tn, sigmoid=False, out_dtype=F32):
    M, K = a.shape
    N = w.shape[1]
    assert M % tm == 0 and N % tn == 0
    return pl.pallas_call(
        functools.partial(_mm_kernel, sigmoid=sigmoid),
        grid=(N // tn, M // tm),
        in_specs=[pl.BlockSpec((tm, K), lambda j, i: (i, 0)),
                  pl.BlockSpec((K, tn), lambda j, i: (0, j))],
        out_specs=pl.BlockSpec((tm, tn), lambda j, i: (i, j)),
        out_shape=jax.ShapeDtypeStruct((M, N), out_dtype),
        scratch_shapes=[pltpu.VMEM((K, tn), BF16)],
        compiler_params=_params(2),
        name="dense_matmul",
    )(a, w)


def _rope(x, cos, sin):
    return x * cos + pltpu.roll(x, HEAD_DIM // 2, 1) * sin


def _q_kernel(a_ref, w_ref, cos_ref, sin_ref, o16_ref, wbf_ref):
    @pl.when(pl.program_id(1) == 0)
    def _():
        wbf_ref[...] = w_ref[...].astype(BF16)

    h = _dot(a_ref[...], wbf_ref[...])
    for u in range(4):
        cols = slice(u * HEAD_DIM, (u + 1) * HEAD_DIM)
        o16_ref[:, cols] = _rope(h[:, cols], cos_ref[...], sin_ref[...]).astype(BF16)


def _kv_kernel(a_ref, w_ref, cos_ref, sin_ref, o32_ref, o16_ref, wbf_ref):
    @pl.when(pl.program_id(1) == 0)
    def _():
        wbf_ref[...] = w_ref[...].astype(BF16)

    h = _dot(a_ref[...], wbf_ref[...])
    tm = h.shape[0]
    for u in range(KV_UNITS):
        cols = slice(u * HEAD_DIM, (u + 1) * HEAD_DIM)
        y = _rope(h[:, cols], cos_ref[...], sin_ref[...]) if u < KV_HEADS else h[:, cols]
        o32_ref[pl.ds(u, tm, stride=KV_UNITS), :] = y
        o16_ref[:, cols] = y.astype(BF16)


def _proj_specs(K, tn):
    return [pl.BlockSpec((ROW_TM, K), lambda j, i: (i, 0)),
            pl.BlockSpec((K, tn), lambda j, i: (0, j)),
            pl.BlockSpec((ROW_TM, HEAD_DIM), lambda j, i: (i, 0)),
            pl.BlockSpec((ROW_TM, HEAD_DIM), lambda j, i: (i, 0))]


def _q_proj(a, w, cos_t, sin_t):
    M, K = a.shape
    N = w.shape[1]
    tn = 4 * HEAD_DIM
    return pl.pallas_call(
        _q_kernel,
        grid=(N // tn, M // ROW_TM),
        in_specs=_proj_specs(K, tn),
        out_specs=pl.BlockSpec((ROW_TM, tn), lambda j, i: (i, j)),
        out_shape=jax.ShapeDtypeStruct((M, N), BF16),
        scratch_shapes=[pltpu.VMEM((K, tn), BF16)],
        compiler_params=_params(2),
        name="q_proj",
    )(a, w, cos_t, sin_t)


def _kv_proj(a, w, cos_t, sin_t):
    M, K = a.shape
    N = w.shape[1]
    tn = 2 * KV_COLS
    return pl.pallas_call(
        _kv_kernel,
        grid=(N // tn, M // ROW_TM),
        in_specs=_proj_specs(K, tn),
        out_specs=[pl.BlockSpec((None, KV_UNITS * ROW_TM, HEAD_DIM), lambda j, i: (j, i, 0)),
                   pl.BlockSpec((ROW_TM, tn), lambda j, i: (i, j))],
        out_shape=[jax.ShapeDtypeStruct((N // tn, KV_UNITS * M, HEAD_DIM), F32), jax.ShapeDtypeStruct((M, N), BF16)],
        scratch_shapes=[pltpu.VMEM((K, tn), BF16)],
        compiler_params=_params(2),
        name="kv_proj",
    )(a, w, cos_t, sin_t)


def _masked_softmax(s, mask):
    s = jnp.where(mask, s, NEG)
    m = jnp.max(s, axis=-1, keepdims=True)
    e = jnp.where(mask, jnp.exp(s - m), 0.0)
    return e / jnp.maximum(jnp.sum(e, axis=-1, keepdims=True), 1e-30)


def _online_update(s, mask, v, m, l, acc):
    s = jnp.where(mask, s, NEG)
    m_new = jnp.maximum(m, jnp.max(s, axis=-1, keepdims=True))
    alpha = jnp.exp(m - m_new)
    p = jnp.where(mask, jnp.exp(s - m_new), 0.0)
    l = alpha * l + jnp.sum(p, axis=-1, keepdims=True)
    acc = alpha * acc + _dot(p.astype(BF16), v)
    return m_new, l, acc


def _online_init(rows):
    return (jnp.full((rows, 1), NEG, F32), jnp.zeros((rows, 1), F32), jnp.zeros((rows, HEAD_DIM), F32))


def _online_finish(l, acc):
    return acc / jnp.maximum(l, 1e-30)


def _select_blocks(imp, tpos, n_sel):
    L = imp.shape[1]
    jidx = lax.broadcasted_iota(jnp.int32, (1, L), 1)
    jf = jidx.astype(F32)
    cur = _div(tpos, SEL_BLK)
    valid = (jidx * SEL_BLK <= tpos) & (jidx < n_sel)
    forced = (jidx == 0) | (jidx == cur) | (jidx == cur - 1)
    score = jnp.where(valid, imp + jnp.where(forced, FORCE_BONUS, 0.0), NEG)
    score = jnp.where(jidx < n_sel, score, BELOW_NEG)
    picked = jnp.zeros(imp.shape, F32)
    for _ in range(min(N_SEL_TOP, n_sel)):
        m = jnp.max(score, axis=-1, keepdims=True)
        first = jnp.min(jnp.where(score == m, jf, float(L)), axis=-1, keepdims=True)
        pick = jf == first
        picked = jnp.where(pick, 1.0, picked)
        score = jnp.where(pick, BELOW_NEG, score)
    return jnp.where(valid, picked, 0.0)


def _overlap_matrix(n_cmp, n_sel, rows, cols):
    i = np.arange(rows)[:, None]
    j = np.arange(cols)[None, :]
    ov = (i * CMP_STRIDE < (j + 1) * SEL_BLK) & (i * CMP_STRIDE + CMP_BLK > j * SEL_BLK) & (i < n_cmp) & (j < n_sel)
    return jnp.asarray(ov.astype(np.float32), dtype=BF16)


def _compress_body(step, n_steps, row_refs, pe_ref, w1_ref, b1_ref, w2_ref, b2_ref, o_ref, x_ref, n_ch):
    cpr = row_refs[0].shape[0] // (KV_UNITS * CMP_STRIDE)
    cps = cpr * len(row_refs)
    for k, ref in enumerate(row_refs):
        base = pl.multiple_of(step * cps + k * cpr, 8)
        for u in range(KV_UNITS):
            for s in range(CMP_STRIDE):
                col = (u * CMP_STRIDE + s) * HEAD_DIM
                x_ref[pl.ds(base, cpr), col:col + HEAD_DIM] = ref[pl.ds(KV_UNITS * s + u, cpr, stride=KV_UNITS * CMP_STRIDE), :]

    @pl.when(step == n_steps - 1)
    def _():
        x_ref[n_ch:n_ch + 16, :] = pe_ref[...]
        width = CMP_STRIDE * HEAD_DIM
        ridx = lax.broadcasted_iota(jnp.int32, (n_ch, 1), 0)
        for u in range(KV_UNITS):
            kk = u // KV_HEADS
            acc = _dot(x_ref[:, u * width:(u + 1) * width].astype(BF16), w1_ref[kk])
            pe_term = acc[n_ch:n_ch + 1, :HEAD_DIM] + acc[n_ch + 1:n_ch + 2, HEAD_DIM:]
            z = acc[:n_ch, :HEAD_DIM] + pltpu.roll(acc[:n_ch, HEAD_DIM:], n_ch - 1, 0) + (b1_ref[kk] + pe_term)
            c = _dot(jax.nn.gelu(z).astype(BF16), w2_ref[kk]) + b2_ref[kk]
            c = jnp.where(ridx < n_ch - (CMP_R - 1), c, 0.0)
            o_ref[:, u * HEAD_DIM:(u + 1) * HEAD_DIM] = c.astype(o_ref.dtype)


def _compress_prompt_kernel(rows_ref, pe_ref, w1_ref, b1_ref, w2_ref, b2_ref, o_ref, x_ref, *, n_ch):
    _compress_body(pl.program_id(1), pl.num_programs(1), [rows_ref], pe_ref, w1_ref, b1_ref, w2_ref, b2_ref,
                   o_ref, x_ref, n_ch)


def _compress_paged_kernel(pt_ref, *refs, n_ch):
    del pt_ref
    pages = list(refs[:PAGES_PER_STEP])
    pe_ref, w1_ref, b1_ref, w2_ref, b2_ref, o_ref, x_ref = refs[PAGES_PER_STEP:]
    _compress_body(pl.program_id(1), pl.num_programs(1), pages, pe_ref, w1_ref, b1_ref, w2_ref, b2_ref,
                   o_ref, x_ref, n_ch)


def _compress_weights(pe, w1, b1, w2, b2):
    hid = w1.shape[-1]
    w1r = w1.reshape(2, CMP_R, CMP_STRIDE, HEAD_DIM, hid)
    w1cat = w1r.transpose(0, 2, 3, 1, 4).reshape(2, CMP_STRIDE * HEAD_DIM, CMP_R * hid).astype(BF16)
    per = pe.reshape(2, CMP_R, CMP_STRIDE * HEAD_DIM)
    per = jnp.broadcast_to(per[:, None], (2, KV_HEADS, CMP_R, CMP_STRIDE * HEAD_DIM))
    per = per.transpose(2, 0, 1, 3).reshape(CMP_R, KV_UNITS * CMP_STRIDE * HEAD_DIM)
    pe_rows = jnp.concatenate([per, jnp.zeros((16 - CMP_R, per.shape[1]), F32)], axis=0)
    return pe_rows, w1cat, b1.reshape(2, 1, hid), w2.astype(BF16), b2.reshape(2, 1, HEAD_DIM)


def _compress_prompt(kvi, B, T, cw):
    R4 = T * KV_UNITS
    n_ch = R4 // (KV_UNITS * CMP_STRIDE)
    n_steps = 2
    pe_rows, w1cat, b1, w2, b2 = cw
    xw = KV_UNITS * CMP_STRIDE * HEAD_DIM
    return pl.pallas_call(
        functools.partial(_compress_prompt_kernel, n_ch=n_ch),
        grid=(B, n_steps),
        in_specs=[pl.BlockSpec((None, R4 // n_steps, HEAD_DIM), lambda b, s: (0, b * n_steps + s, 0)),
                  pl.BlockSpec(pe_rows.shape, lambda b, s: (0, 0)),
                  pl.BlockSpec(w1cat.shape, lambda b, s: (0, 0, 0)),
                  pl.BlockSpec(b1.shape, lambda b, s: (0, 0, 0)),
                  pl.BlockSpec(w2.shape, lambda b, s: (0, 0, 0)),
                  pl.BlockSpec(b2.shape, lambda b, s: (0, 0, 0))],
        out_specs=pl.BlockSpec((None, n_ch, KV_UNITS * HEAD_DIM), lambda b, s: (b, 0, 0)),
        out_shape=jax.ShapeDtypeStruct((B, n_ch, KV_UNITS * HEAD_DIM), BF16),
        scratch_shapes=[pltpu.VMEM((n_ch + 16, xw), F32)],
        compiler_params=_params(2),
        name="compress_prompt",
    )(kvi, pe_rows, w1cat, b1, w2, b2)


def _compress_paged(cache, layer, page_table, cw):
    DB, n_pages = page_table.shape
    assert n_pages % PAGES_PER_STEP == 0
    n_steps = n_pages // PAGES_PER_STEP
    n_ch = n_pages * PAGE_SIZE // CMP_STRIDE
    pe_rows, w1cat, b1, w2, b2 = cw
    xw = KV_UNITS * CMP_STRIDE * HEAD_DIM

    def page_spec(k):
        return pl.BlockSpec((None, None, PAGE_SIZE * KV_UNITS, HEAD_DIM),
                            lambda b, s, pt: (layer, pt[b * n_pages + s * PAGES_PER_STEP + k], 0, 0))

    grid_spec = pltpu.PrefetchScalarGridSpec(
        num_scalar_prefetch=1,
        grid=(DB, n_steps),
        in_specs=[page_spec(k) for k in range(PAGES_PER_STEP)] + [
            pl.BlockSpec(pe_rows.shape, lambda b, s, pt: (0, 0)),
            pl.BlockSpec(w1cat.shape, lambda b, s, pt: (0, 0, 0)),
            pl.BlockSpec(b1.shape, lambda b, s, pt: (0, 0, 0)),
            pl.BlockSpec(w2.shape, lambda b, s, pt: (0, 0, 0)),
            pl.BlockSpec(b2.shape, lambda b, s, pt: (0, 0, 0))],
        out_specs=pl.BlockSpec((None, n_ch, KV_UNITS * HEAD_DIM), lambda b, s, pt: (b, 0, 0)),
        scratch_shapes=[pltpu.VMEM((n_ch + 16, xw), F32)],
    )
    return pl.pallas_call(
        functools.partial(_compress_paged_kernel, n_ch=n_ch),
        grid_spec=grid_spec,
        out_shape=jax.ShapeDtypeStruct((DB, n_ch, KV_UNITS * HEAD_DIM), BF16),
        compiler_params=_params(2),
        name="compress_paged",
    )(page_table.reshape(-1), *([cache] * PAGES_PER_STEP), pe_rows, w1cat, b1, w2, b2)


def _nsa_prompt_kernel(q_ref, ks_ref, kw_ref, kcv_ref, gate_ref, ov_ref, e_ref, o_ref, *, n_cmp, n_sel):
    TQ, TK = NSA_TQ, NSA_TK
    R = Q_PER_KV * TQ
    t0 = pl.program_id(1) * TQ
    tpos = t0 + lax.broadcasted_iota(jnp.int32, (TQ, 1), 0)
    tpos4 = jnp.concatenate([tpos] * Q_PER_KV, axis=0)
    NC = kcv_ref.shape[0]
    cidx = lax.broadcasted_iota(jnp.int32, (1, NC), 1)
    m_c = (cidx * CMP_STRIDE + (CMP_BLK - 1) <= tpos4) & (cidx < n_cmp)
    klane = lax.broadcasted_iota(jnp.int32, (1, TK), 1)
    n_sel_tiles = (t0 + TQ + TK - 1) // TK
    w_lo = jnp.maximum(t0 - WINDOW, 0) // TK
    w_hi = (t0 + TQ - 1) // TK + 1

    for g in range(KV_HEADS):
        q4 = jnp.concatenate([q_ref[:, (Q_PER_KV * g + r) * HEAD_DIM:(Q_PER_KV * g + r + 1) * HEAD_DIM]
                              for r in range(Q_PER_KV)], axis=0)
        kcol = slice(g * HEAD_DIM, (g + 1) * HEAD_DIM)
        vcol = slice((KV_HEADS + g) * HEAD_DIM, (KV_HEADS + g + 1) * HEAD_DIM)

        p_c = _masked_softmax(_dot_nt(q4, kcv_ref[:, kcol]) * ATT_SCALE, m_c)
        o_c = _dot(p_c.astype(BF16), kcv_ref[:, vcol])
        imp4 = _split_dot(p_c, ov_ref[...])
        imp = imp4[0:TQ]
        for r in range(1, Q_PER_KV):
            imp = imp + imp4[r * TQ:(r + 1) * TQ]
        sel = _select_blocks(imp, tpos, n_sel)
        sel4 = jnp.concatenate([sel] * Q_PER_KV, axis=0).astype(BF16)

        def sel_step(kb, carry):
            k0 = pl.multiple_of(kb * TK, TK)
            s = _dot_nt(q4, ks_ref[pl.ds(k0, TK), kcol]) * ATT_SCALE
            in_sel = _dot(sel4, e_ref[kb]) > 0.5
            mask = in_sel & (k0 + klane <= tpos4)
            return _online_update(s, mask, ks_ref[pl.ds(k0, TK), vcol], *carry)

        _, l_s, acc_s = lax.fori_loop(0, n_sel_tiles, sel_step, _online_init(R))
        o_s = _online_finish(l_s, acc_s)

        def win_step(kb, carry):
            k0 = pl.multiple_of(kb * TK, TK)
            s = _dot_nt(q4, kw_ref[pl.ds(k0, TK), kcol]) * ATT_SCALE
            delta = tpos4 - (k0 + klane)
            mask = (delta >= 0) & (delta < WINDOW)
            return _online_update(s, mask, kw_ref[pl.ds(k0, TK), vcol], *carry)

        _, l_w, acc_w = lax.fori_loop(w_lo, w_hi, win_step, _online_init(R))
        o_w = _online_finish(l_w, acc_w)

        for r in range(Q_PER_KV):
            h = Q_PER_KV * g + r
            rows = slice(r * TQ, (r + 1) * TQ)
            o = (gate_ref[:, 3 * h:3 * h + 1] * o_c[rows] + gate_ref[:, 3 * h + 1:3 * h + 2] * o_s[rows]
                 + gate_ref[:, 3 * h + 2:3 * h + 3] * o_w[rows])
            o_ref[:, h * HEAD_DIM:(h + 1) * HEAD_DIM] = o.astype(o_ref.dtype)


def _nsa_prompt(q16, kv16, kcv, gates, B, T):
    TQ, TK = NSA_TQ, NSA_TK
    nq = T // TQ
    NC = kcv.shape[1]
    n_cmp, n_sel = NC - (CMP_R - 1), T // SEL_BLK
    ov = _overlap_matrix(n_cmp, n_sel, NC, LANES)
    kidx = np.arange(T) // SEL_BLK
    e = (np.arange(LANES)[None, :, None] == kidx.reshape(T // TK, 1, TK)).astype(np.float32)
    e = jnp.asarray(e, dtype=BF16)
    kv_w = 2 * KV_COLS
    return pl.pallas_call(
        functools.partial(_nsa_prompt_kernel, n_cmp=n_cmp, n_sel=n_sel),
        grid=(B, nq),
        in_specs=[pl.BlockSpec((TQ, ATT_W), lambda b, i: (b * nq + i, 0)),
                  pl.BlockSpec((T, kv_w), lambda b, i: (b, 1)),
                  pl.BlockSpec((T, kv_w), lambda b, i: (b, 2)),
                  pl.BlockSpec((None, NC, KV_UNITS * HEAD_DIM), lambda b, i: (b, 0, 0)),
                  pl.BlockSpec((TQ, LANES), lambda b, i: (b * nq + i, 0)),
                  pl.BlockSpec(ov.shape, lambda b, i: (0, 0)),
                  pl.BlockSpec(e.shape, lambda b, i: (0, 0, 0))],
        out_specs=pl.BlockSpec((TQ, ATT_W), lambda b, i: (b * nq + i, 0)),
        out_shape=jax.ShapeDtypeStruct((B * T, ATT_W), BF16),
        compiler_params=_params(2),
        name="nsa_prompt",
    )(q16, kv16, kv16, kcv, gates, ov, e)


def _unit_rows(ref, u, n):
    return ref[pl.ds(u, n, stride=KV_UNITS), :]


def _both_heads(ref, kv, n, pad_to=None):
    parts = []
    for g in range(KV_HEADS):
        parts.append(_unit_rows(ref, kv * KV_HEADS + g, n))
        if pad_to is not None and pad_to > n:
            parts.append(jnp.zeros((pad_to - n, HEAD_DIM), F32))
    return jnp.concatenate(parts, axis=0).astype(BF16)


def _log2(n):
    assert n > 0 and n & (n - 1) == 0, n
    return n.bit_length() - 1


def _div(x, n):
    return jnp.right_shift(x, _log2(n))


def _mod(x, n):
    _log2(n)
    return jnp.bitwise_and(x, n - 1)


def _nsa_sample_kernel(pt_ref, *refs, past, n_cmp, n_sel, DS):
    del pt_ref
    pages = refs[:PAGES_PER_STEP]
    (q_ref, kcv_ref, snew_ref, wst_ref, wnew_ref, gate_ref, ov_ref, o_ref,
     sel_ref, oc_ref, m_ref, l_ref, acc_ref) = refs[PAGES_PER_STEP:]
    step = pl.program_id(1)
    R = N_HEADS * DS
    RG = Q_PER_KV * DS
    ridx = lax.broadcasted_iota(jnp.int32, (R, 1), 0)
    tpos = past + _mod(ridx, DS)
    row_g = _div(ridx, RG)
    q = jnp.concatenate([q_ref[:, h * HEAD_DIM:(h + 1) * HEAD_DIM] for h in range(N_HEADS)], axis=0).astype(BF16)

    def same_head(n_lanes, per_head):
        lane = lax.broadcasted_iota(jnp.int32, (1, n_lanes), 1)
        return (_div(lane, per_head) == row_g), _mod(lane, per_head)

    @pl.when(step == 0)
    def _():
        NC = kcv_ref.shape[0]
        kc = jnp.concatenate([kcv_ref[:, g * HEAD_DIM:(g + 1) * HEAD_DIM] for g in range(KV_HEADS)], axis=0)
        vc = jnp.concatenate([kcv_ref[:, (KV_HEADS + g) * HEAD_DIM:(KV_HEADS + g + 1) * HEAD_DIM]
                              for g in range(KV_HEADS)], axis=0)
        same, cidx = same_head(KV_HEADS * NC, NC)
        m_c = same & (cidx * CMP_STRIDE + (CMP_BLK - 1) <= tpos) & (cidx < n_cmp)
        p_c = _masked_softmax(_dot_nt(q, kc) * ATT_SCALE, m_c)
        oc_ref[...] = _dot(p_c.astype(BF16), vc)
        ov2 = jnp.concatenate([ov_ref[...]] * KV_HEADS, axis=0)
        imp_all = _split_dot(p_c, ov2)
        for g in range(KV_HEADS):
            imp = imp_all[g * RG:g * RG + DS]
            for r in range(1, Q_PER_KV):
                imp = imp + imp_all[g * RG + r * DS:g * RG + (r + 1) * DS]
            sel = _select_blocks(imp, tpos[0:DS], n_sel)
            sel_ref[g * RG:(g + 1) * RG, :] = jnp.concatenate([sel] * Q_PER_KV, axis=0)
        m0, l0, a0 = _online_init(R)
        m_ref[...] = m0
        l_ref[...] = l0
        acc_ref[...] = a0

    sel_b = sel_ref[...].astype(BF16)
    L = sel_ref.shape[1]
    jrow = lax.broadcasted_iota(jnp.int32, (L, 1), 0)
    lane = lax.broadcasted_iota(jnp.int32, (1, PAGES_PER_STEP * KV_HEADS * PAGE_SIZE), 1)
    kpos = (step * PAGES_PER_STEP + _div(lane, KV_HEADS * PAGE_SIZE)) * PAGE_SIZE + _mod(lane, PAGE_SIZE)
    same = _mod(_div(lane, PAGE_SIZE), KV_HEADS) == row_g
    expand = jnp.where(jrow == _div(kpos, SEL_BLK), 1.0, 0.0).astype(BF16)
    mask = same & (_dot(sel_b, expand) > 0.5) & (kpos <= tpos)
    k_all = jnp.concatenate([_both_heads(pg, 0, PAGE_SIZE) for pg in pages], axis=0)
    v_all = jnp.concatenate([_both_heads(pg, 1, PAGE_SIZE) for pg in pages], axis=0)
    s = _dot_nt(q, k_all) * ATT_SCALE
    m_ref[...], l_ref[...], acc_ref[...] = _online_update(s, mask, v_all, m_ref[...], l_ref[...], acc_ref[...])

    @pl.when(step == pl.num_programs(1) - 1)
    def _():
        same_n, tn = same_head(KV_HEADS * LANES, LANES)
        same_n = same_n & (tn < DS)
        npos = past + tn
        expand = jnp.where(jrow == _div(npos, SEL_BLK), 1.0, 0.0).astype(BF16)
        mask = same_n & (_dot(sel_b, expand) > 0.5) & (npos <= tpos)
        s = _dot_nt(q, _both_heads(snew_ref, 0, DS, LANES)) * ATT_SCALE
        _, l_s, acc_s = _online_update(s, mask, _both_heads(snew_ref, 1, DS, LANES),
                                       m_ref[...], l_ref[...], acc_ref[...])
        o_s = _online_finish(l_s, acc_s)

        same_w, win = same_head(KV_HEADS * WINDOW, WINDOW)
        wpos = past - WINDOW + win
        delta = tpos - wpos
        mask = same_w & (delta >= 0) & (delta < WINDOW) & (wpos >= 0)
        s = _dot_nt(q, _both_heads(wst_ref, 0, WINDOW)) * ATT_SCALE
        carry_w = _online_update(s, mask, _both_heads(wst_ref, 1, WINDOW), *_online_init(R))
        delta = tpos - npos
        mask = same_n & (delta >= 0) & (delta < WINDOW)
        s = _dot_nt(q, _both_heads(wnew_ref, 0, DS, LANES)) * ATT_SCALE
        _, l_w, acc_w = _online_update(s, mask, _both_heads(wnew_ref, 1, DS, LANES), *carry_w)
        o_w = _online_finish(l_w, acc_w)

        o_c = oc_ref[...]
        for h in range(N_HEADS):
            rows = slice(h * DS, (h + 1) * DS)
            o = (gate_ref[:, 3 * h:3 * h + 1] * o_c[rows] + gate_ref[:, 3 * h + 1:3 * h + 2] * o_s[rows]
                 + gate_ref[:, 3 * h + 2:3 * h + 3] * o_w[rows])
            o_ref[:, h * HEAD_DIM:(h + 1) * HEAD_DIM] = o


def _nsa_sample(q, kcv, sel_new, win_state, win_new, gates, cache_sel, layer, page_table):
    DB, DS, _ = q.shape
    n_pages = page_table.shape[1]
    past = n_pages * PAGE_SIZE
    assert n_pages % PAGES_PER_STEP == 0 and DS % 8 == 0 and SEL_BLK % DS == 0
    n_steps = n_pages // PAGES_PER_STEP
    NC = kcv.shape[1]
    n_cmp = (past + DS) // CMP_STRIDE - (CMP_R - 1)
    assert NC == n_cmp + (CMP_R - 1)
    n_sel = -(-(past + DS) // SEL_BLK)
    L = -(-n_sel // LANES) * LANES
    ov = _overlap_matrix(n_cmp, n_sel, NC, L)
    R = N_HEADS * DS

    def page_spec(k):
        return pl.BlockSpec((None, None, PAGE_SIZE * KV_UNITS, HEAD_DIM),
                            lambda b, s, pt: (layer, pt[b * n_pages + s * PAGES_PER_STEP + k], 0, 0))

    per_b = lambda shape: pl.BlockSpec((None,) + shape, lambda b, s, pt: (b,) + (0,) * len(shape))
    grid_spec = pltpu.PrefetchScalarGridSpec(
        num_scalar_prefetch=1,
        grid=(DB, n_steps),
        in_specs=[page_spec(k) for k in range(PAGES_PER_STEP)] + [
            per_b((DS, ATT_W)),
            per_b((NC, KV_UNITS * HEAD_DIM)),
            per_b((DS * KV_UNITS, HEAD_DIM)),
            pl.BlockSpec((None, None, WINDOW * KV_UNITS, HEAD_DIM), lambda b, s, pt: (layer, b, 0, 0)),
            per_b((DS * KV_UNITS, HEAD_DIM)),
            per_b((DS, LANES)),
            pl.BlockSpec(ov.shape, lambda b, s, pt: (0, 0))],
        out_specs=per_b((DS, ATT_W)),
        scratch_shapes=[pltpu.VMEM((R, L), F32), pltpu.VMEM((R, HEAD_DIM), F32), pltpu.VMEM((R, 1), F32),
                        pltpu.VMEM((R, 1), F32), pltpu.VMEM((R, HEAD_DIM), F32)],
    )
    return pl.pallas_call(
        functools.partial(_nsa_sample_kernel, past=past, n_cmp=n_cmp, n_sel=n_sel, DS=DS),
        grid_spec=grid_spec,
        out_shape=jax.ShapeDtypeStruct((DB, DS, ATT_W), F32),
        compiler_params=_params(2),
        name="nsa_sample",
    )(page_table.reshape(-1), *([cache_sel] * PAGES_PER_STEP), q, kcv, sel_new, win_state, win_new, gates, ov)


def _moe_up_kernel(be_ref, bv_ref, x_ref, w_ref, b_ref, pick_ref, o_ref, wbf_ref):
    i = pl.program_id(1)
    prev_e = be_ref[jnp.maximum(i - 1, 0)]

    @pl.when((i == 0) | (be_ref[i] != prev_e))
    def _():
        wbf_ref[...] = w_ref[...].astype(BF16)

    @pl.when(bv_ref[i] != 0)
    def _():
        h = _dot(x_ref[...], wbf_ref[...]) + b_ref[...]
        g = jnp.minimum(h, SWIGLU_LIMIT)
        lin = jnp.clip(h, -SWIGLU_LIMIT, SWIGLU_LIMIT) + 1.0
        t = g * jax.nn.sigmoid(SWIGLU_ALPHA * g) * pltpu.roll(lin, h.shape[1] - 1, 1)
        o_ref[...] = _dot(t.astype(BF16), pick_ref[...]).astype(o_ref.dtype)

    @pl.when(bv_ref[i] == 0)
    def _():
        o_ref[...] = jnp.zeros_like(o_ref)


def _moe_up(block_e, block_valid, xs, w_up, b_up, tn):
    rows, D = xs.shape
    F = w_up.shape[2] // 2
    nb = rows // MOE_TM
    pick = jnp.asarray(np.arange(2 * tn)[:, None] == 2 * np.arange(tn)[None, :], dtype=BF16)
    grid_spec = pltpu.PrefetchScalarGridSpec(
        num_scalar_prefetch=2,
        grid=(F // tn, nb),
        in_specs=[pl.BlockSpec((MOE_TM, D), lambda j, i, be, bv: (i, 0)),
                  pl.BlockSpec((None, D, 2 * tn), lambda j, i, be, bv: (be[i], 0, j)),
                  pl.BlockSpec((None, 1, 2 * tn), lambda j, i, be, bv: (be[i], 0, j)),
                  pl.BlockSpec((2 * tn, tn), lambda j, i, be, bv: (0, 0))],
        out_specs=pl.BlockSpec((MOE_TM, tn), lambda j, i, be, bv: (i, j)),
        scratch_shapes=[pltpu.VMEM((D, 2 * tn), BF16)],
    )
    return pl.pallas_call(
        _moe_up_kernel,
        grid_spec=grid_spec,
        out_shape=jax.ShapeDtypeStruct((rows, F), BF16),
        compiler_params=_params(2),
        name="moe_up",
    )(block_e, block_valid, xs, w_up, b_up, pick)


def _moe_down_kernel(be_ref, bv_ref, a_ref, w_ref, b_ref, rw_ref, o_ref, wbf_ref):
    i = pl.program_id(1)
    prev_e = be_ref[jnp.maximum(i - 1, 0)]

    @pl.when((i == 0) | (be_ref[i] != prev_e))
    def _():
        wbf_ref[...] = w_ref[...].astype(BF16)

    @pl.when(bv_ref[i] != 0)
    def _():
        h = _dot(a_ref[...], wbf_ref[...]) + b_ref[...]
        o_ref[...] = h * rw_ref[...]

    @pl.when(bv_ref[i] == 0)
    def _():
        o_ref[...] = jnp.zeros_like(o_ref)


def _moe_down(block_e, block_valid, act, w_down, b_down, row_w, tn):
    rows, F = act.shape
    D = w_down.shape[2]
    nb = rows // MOE_TM
    grid_spec = pltpu.PrefetchScalarGridSpec(
        num_scalar_prefetch=2,
        grid=(D // tn, nb),
        in_specs=[pl.BlockSpec((MOE_TM, F), lambda j, i, be, bv: (i, 0)),
                  pl.BlockSpec((None, F, tn), lambda j, i, be, bv: (be[i], 0, j)),
                  pl.BlockSpec((None, 1, tn), lambda j, i, be, bv: (be[i], 0, j)),
                  pl.BlockSpec((MOE_TM, 1), lambda j, i, be, bv: (i, 0))],
        out_specs=pl.BlockSpec((MOE_TM, tn), lambda j, i, be, bv: (i, j)),
        scratch_shapes=[pltpu.VMEM((F, tn), BF16)],
    )
    return pl.pallas_call(
        _moe_down_kernel,
        grid_spec=grid_spec,
        out_shape=jax.ShapeDtypeStruct((rows, D), F32),
        compiler_params=_params(2),
        name="moe_down",
    )(block_e, block_valid, act, w_down, b_down, row_w)


def _moe(x, router_w, router_b, w_up, b_up, w_down, b_down):
    N, D = x.shape
    logits = jnp.dot(x, router_w, precision=HI) + router_b
    top_v, top_e = lax.top_k(logits, TOP_K)
    gate = jax.nn.softmax(top_v, axis=-1)
    M = N * TOP_K
    flat_e = top_e.reshape(M)
    order = jnp.argsort(flat_e).astype(jnp.int32)
    counts = jnp.sum(flat_e[:, None] == jnp.arange(N_EXPERTS, dtype=flat_e.dtype)[None, :], axis=0).astype(jnp.int32)
    padded = (counts + MOE_TM - 1) // MOE_TM * MOE_TM
    start = jnp.cumsum(counts) - counts
    pend = jnp.cumsum(padded)
    pstart = pend - padded
    nb = -(-(M + N_EXPERTS * (MOE_TM - 1)) // MOE_TM)
    rows = nb * MOE_TM
    blk0 = jnp.arange(nb, dtype=jnp.int32) * MOE_TM
    block_e = jnp.clip(jnp.searchsorted(pend, blk0, side='right'), 0, N_EXPERTS - 1).astype(jnp.int32)
    block_valid = (blk0 < pend[-1]).astype(jnp.int32)
    p = jnp.arange(rows, dtype=jnp.int32)
    e_p = block_e[p // MOE_TM]
    q = p - pstart[e_p]
    live = q < counts[e_p]
    src = order[jnp.clip(start[e_p] + q, 0, M - 1)]
    row_tok = jnp.where(live, src // TOP_K, N).astype(jnp.int32)
    row_w = jnp.where(live, gate.reshape(M)[src], 0.0)
    slot = pstart[flat_e] + jnp.argsort(order).astype(jnp.int32) - start[flat_e]

    x_pad = jnp.concatenate([x.astype(BF16), jnp.zeros((1, D), BF16)], axis=0)
    xs = x_pad[row_tok]
    act = _moe_up(block_e, block_valid, xs, w_up, b_up.reshape(N_EXPERTS, 1, 2 * D_FF), tn=512)
    out = _moe_down(block_e, block_valid, act, w_down, b_down.reshape(N_EXPERTS, 1, D), row_w[:, None], tn=1024)
    return out[slot.reshape(N, TOP_K).T.reshape(M)]


def _residual_ln_kernel(*refs, n_add):
    adds, (x_ref, g_ref, b_ref, o_ref) = refs[:n_add], refs[n_add:]
    y = adds[0][...]
    for a in adds[1:]:
        y = y + a[...]
    z = DN_ALPHA * x_ref[...] + y
    mu = jnp.mean(z, axis=-1, keepdims=True)
    var = jnp.mean(jnp.square(z - mu), axis=-1, keepdims=True)
    o_ref[...] = (z - mu) * lax.rsqrt(var + LN_EPS) * g_ref[...] + b_ref[...]


def _residual_ln(stacked, n_add, x, g, b):
    N, D = x.shape
    tm = 256
    nt = N // tm
    assert N % tm == 0 and stacked.shape == (n_add * N, D)
    return pl.pallas_call(
        functools.partial(_residual_ln_kernel, n_add=n_add),
        grid=(nt,),
        in_specs=[pl.BlockSpec((tm, D), functools.partial(lambda i, k: (k * nt + i, 0), k=k)) for k in range(n_add)] + [
            pl.BlockSpec((tm, D), lambda i: (i, 0)),
            pl.BlockSpec((1, D), lambda i: (0, 0)),
            pl.BlockSpec((1, D), lambda i: (0, 0))],
        out_specs=pl.BlockSpec((tm, D), lambda i: (i, 0)),
        out_shape=jax.ShapeDtypeStruct((N, D), F32),
        compiler_params=_params(1),
        name="residual_ln",
    )(*([stacked] * n_add), x, g.reshape(1, D), b.reshape(1, D))


def _conv_body(u, hu, w_ref, b_ref, g_ref, beta_ref, o_ref, tail_ref, s_ref, y_ref):
    TT = u.shape[0]
    s_ref[0:CONV_HALO, :] = hu
    s_ref[CONV_HALO:CONV_HALO + TT, :] = u
    off = CONV_HALO - (CONV_W - 1)
    for cc in range(C_CONV // LANES):
        cols = slice(cc * LANES, (cc + 1) * LANES)
        acc = jnp.broadcast_to(b_ref[:, cols], (TT, LANES))
        for j in range(CONV_W):
            acc = acc + s_ref[pl.ds(j + off, TT), cols] * w_ref[j:j + 1, cols]
        y_ref[:, cols] = acc
    y = y_ref[...]
    mu = jnp.mean(y, axis=-1, keepdims=True)
    var = jnp.mean(jnp.square(y - mu), axis=-1, keepdims=True)
    yn = (y - mu) * lax.rsqrt(var + LN_EPS) * g_ref[...] + beta_ref[...]
    o_ref[...] = (yn * jax.nn.sigmoid(yn)).astype(o_ref.dtype)
    tail_ref[...] = s_ref[TT:TT + CONV_HALO, :]


def _glu(ref):
    return ref[:, :C_CONV] * jax.nn.sigmoid(ref[:, C_CONV:])


def _conv_prompt_kernel(main_ref, halo_ref, *rest):
    hu = jnp.where(pl.program_id(1) > 0, _glu(halo_ref), 0.0)
    _conv_body(_glu(main_ref), hu, *rest)


def _conv_sample_kernel(main_ref, state_ref, *rest):
    _conv_body(_glu(main_ref), state_ref[...], *rest)


def _conv_call(kern, grid, in_specs, out_rows_spec, n_seq, TT, n_rows, out_dtype, name, args):
    C = C_CONV
    const = lambda shape: pl.BlockSpec(shape, lambda b, i: (0, 0))
    return pl.pallas_call(
        kern,
        grid=grid,
        in_specs=in_specs + [const((CONV_W, C)), const((1, C)), const((1, C)), const((1, C))],
        out_specs=[out_rows_spec, pl.BlockSpec((None, CONV_HALO, C), lambda b, i: (b, 0, 0))],
        out_shape=[jax.ShapeDtypeStruct((n_rows, C), out_dtype), jax.ShapeDtypeStruct((n_seq, CONV_HALO, C), F32)],
        scratch_shapes=[pltpu.VMEM((CONV_HALO + TT, C), F32), pltpu.VMEM((TT, C), F32)],
        compiler_params=_params(2),
        name=name,
    )(*args)


def _conv_prompt(cin, B, T, conv_w, conv_b, ln_g, ln_b):
    TT = 128
    nt, hb = T // TT, TT // CONV_HALO
    C = C_CONV
    in_specs = [pl.BlockSpec((TT, 2 * C), lambda b, i: (b * nt + i, 0)),
                pl.BlockSpec((CONV_HALO, 2 * C), lambda b, i: (jnp.maximum((b * nt + i) * hb - 1, 0), 0))]
    return _conv_call(_conv_prompt_kernel, (B, nt), in_specs, pl.BlockSpec((TT, C), lambda b, i: (b * nt + i, 0)),
                      B, TT, B * T, BF16, "conv_prompt",
                      (cin, cin, conv_w, conv_b.reshape(1, C), ln_g.reshape(1, C), ln_b.reshape(1, C)))


def _conv_sample(cin, row0, DB, DS, state, conv_w, conv_b, ln_g, ln_b):
    C = C_CONV
    assert row0 % DS == 0
    in_specs = [pl.BlockSpec((DS, 2 * C), lambda b, i: (row0 // DS + b, 0)),
                pl.BlockSpec((None, CONV_HALO, C), lambda b, i: (b, 0, 0))]
    return _conv_call(_conv_sample_kernel, (DB, 1), in_specs, pl.BlockSpec((DS, C), lambda b, i: (b, 0)),
                      DB, DS, DB * DS, F32, "conv_sample",
                      (cin, state, conv_w, conv_b.reshape(1, C), ln_g.reshape(1, C), ln_b.reshape(1, C)))


def _rope_tables(pos):
    inv = ROPE_THETA ** (-jnp.arange(0, HEAD_DIM, 2, dtype=F32) / HEAD_DIM)
    ang = pos.astype(F32)[:, None] * inv[None, :]
    cos, sin = jnp.cos(ang), jnp.sin(ang)
    return jnp.concatenate([cos, cos], axis=-1), jnp.concatenate([-sin, sin], axis=-1)


def kernel(x_prompt, x_sample, cache_cmp_kv, cache_sel_kv, state_win_kv, state_conv, page_table,
           w_in, cmp_pe, cmp_w1, cmp_b1, cmp_w2, cmp_b2, conv_w, conv_b, conv_ln_g, conv_ln_b,
           w_out, ln1_g, ln1_b, router_w, router_b, exp_w_up, exp_b_up, exp_w_down, exp_b_down,
           ln2_g, ln2_b):
    B, T, _ = x_prompt.shape
    DB, DS, _ = x_sample.shape
    n_pages = page_table.shape[1]
    n_pool = cache_cmp_kv.shape[1]
    past = n_pages * PAGE_SIZE
    NP, NS = B * T, DB * DS
    pos = jnp.concatenate([jnp.tile(jnp.arange(T, dtype=jnp.int32), B),
                           jnp.tile(past + jnp.arange(DS, dtype=jnp.int32), DB)])
    cos_t, sin_t = _rope_tables(pos)
    cache_cmp = cache_cmp_kv.reshape(DEPTH, n_pool, PAGE_SIZE * KV_UNITS, HEAD_DIM)
    cache_sel = cache_sel_kv.reshape(DEPTH, n_pool, PAGE_SIZE * KV_UNITS, HEAD_DIM)
    win_state = state_win_kv.reshape(DEPTH, DB, WINDOW * KV_UNITS, HEAD_DIM)
    kv_shape = (2, KV_HEADS, HEAD_DIM)
    conv_state = jnp.pad(state_conv, ((0, 0), (0, 0), (CONV_HALO - (CONV_W - 1), 0), (0, 0)))
    x = jnp.concatenate([x_prompt.reshape(NP, D_MODEL), x_sample.reshape(NS, D_MODEL)], axis=0)
    outs = [[] for _ in range(8)]
    p4, s4 = NP * KV_UNITS, NS * KV_UNITS
    for l in range(DEPTH):
        xb = x.astype(BF16)
        w_gate = jnp.pad(w_in[l][:, QKV_COLS:QKV_COLS + GATE_COLS], ((0, 0), (0, LANES - GATE_COLS)))
        q16 = _q_proj(xb, w_in[l][:, :ATT_W], cos_t, sin_t)
        kvi, kv16 = _kv_proj(xb, w_in[l][:, ATT_W:QKV_COLS], cos_t, sin_t)
        cin = _matmul(xb, w_in[l][:, QKV_COLS + GATE_COLS:], tm=ROW_TM, tn=512)
        gates = _matmul(xb, w_gate, tm=ROW_TM, tn=LANES, sigmoid=True)
        cw = _compress_weights(cmp_pe[l], cmp_w1[l], cmp_b1[l], cmp_w2[l], cmp_b2[l])
        kv_s = kvi[:, p4:p4 + s4].reshape(3, DB, DS * KV_UNITS, HEAD_DIM)

        kcv_p = _compress_prompt(kvi, B, T, cw)
        att_p = _nsa_prompt(q16, kv16, kcv_p, gates, B, T)
        conv_p, tail_p = _conv_prompt(cin, B, T, conv_w[l], conv_b[l], conv_ln_g[l], conv_ln_b[l])
        assert (past + DS) // CMP_STRIDE == past // CMP_STRIDE
        kcv_s = _compress_paged(cache_cmp, l, page_table, cw)
        att_s = _nsa_sample(q16[NP:].astype(F32).reshape(DB, DS, ATT_W), kcv_s, kv_s[1], win_state, kv_s[2],
                            gates[NP:].reshape(DB, DS, LANES), cache_sel, l, page_table)
        conv_s, tail_s = _conv_sample(cin, NP, DB, DS, conv_state[l], conv_w[l], conv_b[l], conv_ln_g[l], conv_ln_b[l])

        kv_p = kvi[:, :p4].reshape((3, B, T) + kv_shape)
        outs[0].append(kv_p[0])
        outs[1].append(kv_p[1])
        outs[2].append(kv_p[2][:, -WINDOW:])
        outs[3].append(tail_p[:, CONV_HALO - (CONV_W - 1):])
        outs[4].append(kv_s[0].reshape((DB, DS) + kv_shape))
        outs[5].append(kv_s[1].reshape((DB, DS) + kv_shape))
        outs[6].append(jnp.concatenate([state_win_kv[l], kv_s[2].reshape((DB, DS) + kv_shape)], axis=1)[:, -WINDOW:])
        outs[7].append(tail_s[:, CONV_HALO - (CONV_W - 1):])

        att = jnp.concatenate([att_p, att_s.reshape(NS, ATT_W).astype(BF16)], axis=0)
        conv_out = jnp.concatenate([conv_p, conv_s.astype(BF16)], axis=0)
        mix = _matmul(jnp.concatenate([att, conv_out], axis=1), w_out[l], tm=ROW_TM, tn=512)
        x = _residual_ln(mix, 1, x, ln1_g[l], ln1_b[l])
        y4 = _moe(x, router_w[l], router_b[l], exp_w_up[l], exp_b_up[l], exp_w_down[l], exp_b_down[l])
        x = _residual_ln(y4, TOP_K, x, ln2_g[l], ln2_b[l])

    return (x[:NP].reshape(B, T, D_MODEL), x[NP:].reshape(DB, DS, D_MODEL)) + tuple(jnp.stack(o) for o in outs)
```

```python
import functools

import jax
import jax.numpy as jnp
import numpy as np
from jax import lax
from jax.experimental import pallas as pl
from jax.experimental.pallas import tpu as pltpu

D_MODEL = 2048
DEPTH = 2
PAGE_SIZE = 128
N_HEADS = 8
HEAD_DIM = 128
KV_HEADS = 2
Q_PER_KV = N_HEADS // KV_HEADS
ATT_W = N_HEADS * HEAD_DIM
KV_COLS = KV_HEADS * HEAD_DIM
KV_UNITS = 2 * KV_HEADS
CMP_BLK = 32
CMP_STRIDE = 16
CMP_R = CMP_BLK // CMP_STRIDE
SEL_BLK = 64
N_SEL_TOP = 16
WINDOW = 512
FORCE_BONUS = 1000.0
NEG = -1e30
BELOW_NEG = -3e38
ROPE_THETA = 10000.0
ATT_SCALE = HEAD_DIM ** -0.5
C_CONV = D_MODEL - ATT_W
CONV_W = 31
QKV_COLS = ATT_W + 6 * KV_COLS
GATE_COLS = 3 * N_HEADS
N_EXPERTS = 32
TOP_K = 4
D_FF = 2048
SWIGLU_LIMIT = 7.0
SWIGLU_ALPHA = 1.702
LN_EPS = 1e-5
DN_ALPHA = (2 * DEPTH) ** 0.25

LANES = 128
VMEM_LIMIT = 56 * 1024 * 1024
MOE_TM = 256
ROW_TM = 768
NSA_TQ = 128
NSA_TK = 256
PAGES_PER_STEP = 8
CONV_HALO = 32
HI = lax.Precision.HIGHEST
F32 = jnp.float32
BF16 = jnp.bfloat16


def _params(n_axes):
    return pltpu.CompilerParams(dimension_semantics=("arbitrary",) * n_axes, vmem_limit_bytes=VMEM_LIMIT)


def _dot(a, b):
    return jnp.dot(a, b, preferred_element_type=F32)


def _dot_nt(a, b):
    return lax.dot_general(a, b, (((1,), (1,)), ((), ())), preferred_element_type=F32)


def _split_dot(p, w):
    hi = p.astype(BF16)
    r1 = p - hi.astype(F32)
    mid = r1.astype(BF16)
    lo = (r1 - mid.astype(F32)).astype(BF16)
    return _dot(hi, w) + _dot(mid, w) + _dot(lo, w)


def _mm_kernel(a_ref, w_ref, o_ref, wbf_ref, *, sigmoid):
    @pl.when(pl.program_id(1) == 0)
    def _():
        wbf_ref[...] = w_ref[...].astype(BF16)

    h = _dot(a_ref[...], wbf_ref[...])
    o_ref[...] = (jax.nn.sigmoid(h) if sigmoid else h).astype(o_ref.dtype)


def _w_spec(w, layer, K, tn, col0=0):
    if layer is None:
        return pl.BlockSpec((K, tn), lambda j, i: (0, col0 + j))
    assert w.ndim == 3
    return pl.BlockSpec((None, K, tn), lambda j, i: (layer, 0, col0 + j))


def _matmul(a, w, tm, tn, sigmoid=False, out_dtype=F32, layer=None):
    M, K = a.shape
    N = w.shape[-1]
    assert M % tm == 0 and N % tn == 0
    return pl.pallas_call(
        functools.partial(_mm_kernel, sigmoid=sigmoid),
        grid=(N // tn, M // tm),
        in_specs=[pl.BlockSpec((tm, K), lambda j, i: (i, 0)),
                  _w_spec(w, layer, K, tn)],
        out_specs=pl.BlockSpec((tm, tn), lambda j, i: (i, j)),
        out_shape=jax.ShapeDtypeStruct((M, N), out_dtype),
        scratch_shapes=[pltpu.VMEM((K, tn), BF16)],
        compiler_params=_params(2),
        name="dense_matmul",
    )(a, w)


def _rope(x, cos, sin):
    return x * cos + pltpu.roll(x, HEAD_DIM // 2, 1) * sin


def _q_kernel(a_ref, w_ref, cos_ref, sin_ref, o16_ref, wbf_ref):
    @pl.when(pl.program_id(1) == 0)
    def _():
        wbf_ref[...] = w_ref[...].astype(BF16)

    h = _dot(a_ref[...], wbf_ref[...])
    for u in range(4):
        cols = slice(u * HEAD_DIM, (u + 1) * HEAD_DIM)
        o16_ref[:, cols] = _rope(h[:, cols], cos_ref[...], sin_ref[...]).astype(BF16)


def _kv_kernel(a_ref, w_ref, cos_ref, sin_ref, o32_ref, o16_ref, wbf_ref):
    @pl.when(pl.program_id(1) == 0)
    def _():
        wbf_ref[...] = w_ref[...].astype(BF16)

    h = _dot(a_ref[...], wbf_ref[...])
    tm = h.shape[0]
    for u in range(KV_UNITS):
        cols = slice(u * HEAD_DIM, (u + 1) * HEAD_DIM)
        y = _rope(h[:, cols], cos_ref[...], sin_ref[...]) if u < KV_HEADS else h[:, cols]
        o32_ref[pl.ds(u, tm, stride=KV_UNITS), :] = y
        o16_ref[:, cols] = y.astype(BF16)


def _proj_specs(w, layer, K, tn, col0):
    return [pl.BlockSpec((ROW_TM, K), lambda j, i: (i, 0)),
            _w_spec(w, layer, K, tn, col0),
            pl.BlockSpec((ROW_TM, HEAD_DIM), lambda j, i: (i, 0)),
            pl.BlockSpec((ROW_TM, HEAD_DIM), lambda j, i: (i, 0))]


def _q_proj(a, w_in, layer, cos_t, sin_t):
    M, K = a.shape
    N = ATT_W
    tn = 4 * HEAD_DIM
    return pl.pallas_call(
        _q_kernel,
        grid=(N // tn, M // ROW_TM),
        in_specs=_proj_specs(w_in, layer, K, tn, 0),
        out_specs=pl.BlockSpec((ROW_TM, tn), lambda j, i: (i, j)),
        out_shape=jax.ShapeDtypeStruct((M, N), BF16),
        scratch_shapes=[pltpu.VMEM((K, tn), BF16)],
        compiler_params=_params(2),
        name="q_proj",
    )(a, w_in, cos_t, sin_t)


def _kv_proj(a, w_in, layer, cos_t, sin_t):
    M, K = a.shape
    N = 6 * KV_COLS
    tn = 2 * KV_COLS
    return pl.pallas_call(
        _kv_kernel,
        grid=(N // tn, M // ROW_TM),
        in_specs=_proj_specs(w_in, layer, K, tn, ATT_W // tn),
        out_specs=[pl.BlockSpec((None, KV_UNITS * ROW_TM, HEAD_DIM), lambda j, i: (j, i, 0)),
                   pl.BlockSpec((ROW_TM, tn), lambda j, i: (i, j))],
        out_shape=[jax.ShapeDtypeStruct((N // tn, KV_UNITS * M, HEAD_DIM), F32), jax.ShapeDtypeStruct((M, N), BF16)],
        scratch_shapes=[pltpu.VMEM((K, tn), BF16)],
        compiler_params=_params(2),
        name="kv_proj",
    )(a, w_in, cos_t, sin_t)


def _masked_softmax(s, mask):
    s = jnp.where(mask, s, NEG)
    m = jnp.max(s, axis=-1, keepdims=True)
    e = jnp.where(mask, jnp.exp(s - m), 0.0)
    return e / jnp.maximum(jnp.sum(e, axis=-1, keepdims=True), 1e-30)


def _online_update(s, mask, v, m, l, acc):
    s = jnp.where(mask, s, NEG)
    m_new = jnp.maximum(m, jnp.max(s, axis=-1, keepdims=True))
    alpha = jnp.exp(m - m_new)
    p = jnp.where(mask, jnp.exp(s - m_new), 0.0)
    l = alpha * l + jnp.sum(p, axis=-1, keepdims=True)
    acc = alpha * acc + _dot(p.astype(BF16), v)
    return m_new, l, acc


def _online_init(rows):
    return (jnp.full((rows, 1), NEG, F32), jnp.zeros((rows, 1), F32), jnp.zeros((rows, HEAD_DIM), F32))


def _online_finish(l, acc):
    return acc / jnp.maximum(l, 1e-30)


def _select_blocks(imp, tpos, n_sel):
    L = imp.shape[1]
    jidx = lax.broadcasted_iota(jnp.int32, (1, L), 1)
    cur = _div(tpos, SEL_BLK)
    valid = (jidx * SEL_BLK <= tpos) & (jidx < n_sel)
    forced = (jidx == 0) | (jidx == cur) | (jidx == cur - 1)
    score = jnp.where(valid, imp + jnp.where(forced, FORCE_BONUS, 0.0), NEG)
    score = jnp.where(jidx < n_sel, score, BELOW_NEG)
    picked = jnp.zeros(imp.shape, F32)
    for _ in range(min(N_SEL_TOP, n_sel)):
        pick = jidx == jnp.argmax(score, axis=-1, keepdims=True)
        picked = jnp.where(pick, 1.0, picked)
        score = jnp.where(pick, BELOW_NEG, score)
    return jnp.where(valid, picked, 0.0)


def _overlap_matrix(n_cmp, n_sel, rows, cols):
    i = np.arange(rows)[:, None]
    j = np.arange(cols)[None, :]
    ov = (i * CMP_STRIDE < (j + 1) * SEL_BLK) & (i * CMP_STRIDE + CMP_BLK > j * SEL_BLK) & (i < n_cmp) & (j < n_sel)
    return jnp.asarray(ov.astype(np.float32), dtype=BF16)


def _compress_body(step, n_steps, row_refs, pe_ref, w1_ref, b1_ref, w2_ref, b2_ref, o_ref, x_ref, n_ch):
    cpr = row_refs[0].shape[0] // (KV_UNITS * CMP_STRIDE)
    cps = cpr * len(row_refs)
    for k, ref in enumerate(row_refs):
        base = pl.multiple_of(step * cps + k * cpr, 8)
        for u in range(KV_UNITS):
            for s in range(CMP_STRIDE):
                col = (u * CMP_STRIDE + s) * HEAD_DIM
                x_ref[pl.ds(base, cpr), col:col + HEAD_DIM] = ref[pl.ds(KV_UNITS * s + u, cpr, stride=KV_UNITS * CMP_STRIDE), :]

    @pl.when(step == n_steps - 1)
    def _():
        x_ref[n_ch:n_ch + 16, :] = pe_ref[...]
        width = CMP_STRIDE * HEAD_DIM
        ridx = lax.broadcasted_iota(jnp.int32, (n_ch, 1), 0)
        for u in range(KV_UNITS):
            kk = u // KV_HEADS
            acc = _dot(x_ref[:, u * width:(u + 1) * width].astype(BF16), w1_ref[kk])
            pe_term = acc[n_ch:n_ch + 1, :HEAD_DIM] + acc[n_ch + 1:n_ch + 2, HEAD_DIM:]
            z = acc[:n_ch, :HEAD_DIM] + pltpu.roll(acc[:n_ch, HEAD_DIM:], n_ch - 1, 0) + (b1_ref[kk] + pe_term)
            c = _dot(jax.nn.gelu(z).astype(BF16), w2_ref[kk]) + b2_ref[kk]
            c = jnp.where(ridx < n_ch - (CMP_R - 1), c, 0.0)
            o_ref[:, u * HEAD_DIM:(u + 1) * HEAD_DIM] = c.astype(o_ref.dtype)


def _compress_prompt_kernel(rows_ref, pe_ref, w1_ref, b1_ref, w2_ref, b2_ref, o_ref, x_ref, *, n_ch):
    _compress_body(pl.program_id(1), pl.num_programs(1), [rows_ref], pe_ref, w1_ref, b1_ref, w2_ref, b2_ref,
                   o_ref, x_ref, n_ch)


def _compress_paged_kernel(pt_ref, *refs, n_ch):
    del pt_ref
    pages = list(refs[:PAGES_PER_STEP])
    pe_ref, w1_ref, b1_ref, w2_ref, b2_ref, o_ref, x_ref = refs[PAGES_PER_STEP:]
    _compress_body(pl.program_id(1), pl.num_programs(1), pages, pe_ref, w1_ref, b1_ref, w2_ref, b2_ref,
                   o_ref, x_ref, n_ch)


def _compress_weights(pe, w1, b1, w2, b2):
    hid = w1.shape[-1]
    w1r = w1.reshape(2, CMP_R, CMP_STRIDE, HEAD_DIM, hid)
    w1cat = w1r.transpose(0, 2, 3, 1, 4).reshape(2, CMP_STRIDE * HEAD_DIM, CMP_R * hid).astype(BF16)
    per = pe.reshape(2, CMP_R, CMP_STRIDE * HEAD_DIM)
    per = jnp.broadcast_to(per[:, None], (2, KV_HEADS, CMP_R, CMP_STRIDE * HEAD_DIM))
    per = per.transpose(2, 0, 1, 3).reshape(CMP_R, KV_UNITS * CMP_STRIDE * HEAD_DIM)
    pe_rows = jnp.concatenate([per, jnp.zeros((16 - CMP_R, per.shape[1]), F32)], axis=0)
    return pe_rows, w1cat, b1.reshape(2, 1, hid), w2.astype(BF16), b2.reshape(2, 1, HEAD_DIM)


def _compress_prompt(kvi, B, T, cw):
    R4 = T * KV_UNITS
    n_ch = R4 // (KV_UNITS * CMP_STRIDE)
    n_steps = 2
    pe_rows, w1cat, b1, w2, b2 = cw
    xw = KV_UNITS * CMP_STRIDE * HEAD_DIM
    return pl.pallas_call(
        functools.partial(_compress_prompt_kernel, n_ch=n_ch),
        grid=(B, n_steps),
        in_specs=[pl.BlockSpec((None, R4 // n_steps, HEAD_DIM), lambda b, s: (0, b * n_steps + s, 0)),
                  pl.BlockSpec(pe_rows.shape, lambda b, s: (0, 0)),
                  pl.BlockSpec(w1cat.shape, lambda b, s: (0, 0, 0)),
                  pl.BlockSpec(b1.shape, lambda b, s: (0, 0, 0)),
                  pl.BlockSpec(w2.shape, lambda b, s: (0, 0, 0)),
                  pl.BlockSpec(b2.shape, lambda b, s: (0, 0, 0))],
        out_specs=pl.BlockSpec((None, n_ch, KV_UNITS * HEAD_DIM), lambda b, s: (b, 0, 0)),
        out_shape=jax.ShapeDtypeStruct((B, n_ch, KV_UNITS * HEAD_DIM), BF16),
        scratch_shapes=[pltpu.VMEM((n_ch + 16, xw), F32)],
        compiler_params=_params(2),
        name="compress_prompt",
    )(kvi, pe_rows, w1cat, b1, w2, b2)


def _compress_paged(cache, layer, page_table, cw):
    DB, n_pages = page_table.shape
    assert n_pages % PAGES_PER_STEP == 0
    n_steps = n_pages // PAGES_PER_STEP
    n_ch = n_pages * PAGE_SIZE // CMP_STRIDE
    pe_rows, w1cat, b1, w2, b2 = cw
    xw = KV_UNITS * CMP_STRIDE * HEAD_DIM

    def page_spec(k):
        return pl.BlockSpec((None, None, PAGE_SIZE * KV_UNITS, HEAD_DIM),
                            lambda b, s, pt: (layer, pt[b * n_pages + s * PAGES_PER_STEP + k], 0, 0))

    grid_spec = pltpu.PrefetchScalarGridSpec(
        num_scalar_prefetch=1,
        grid=(DB, n_steps),
        in_specs=[page_spec(k) for k in range(PAGES_PER_STEP)] + [
            pl.BlockSpec(pe_rows.shape, lambda b, s, pt: (0, 0)),
            pl.BlockSpec(w1cat.shape, lambda b, s, pt: (0, 0, 0)),
            pl.BlockSpec(b1.shape, lambda b, s, pt: (0, 0, 0)),
            pl.BlockSpec(w2.shape, lambda b, s, pt: (0, 0, 0)),
            pl.BlockSpec(b2.shape, lambda b, s, pt: (0, 0, 0))],
        out_specs=pl.BlockSpec((None, n_ch, KV_UNITS * HEAD_DIM), lambda b, s, pt: (b, 0, 0)),
        scratch_shapes=[pltpu.VMEM((n_ch + 16, xw), F32)],
    )
    return pl.pallas_call(
        functools.partial(_compress_paged_kernel, n_ch=n_ch),
        grid_spec=grid_spec,
        out_shape=jax.ShapeDtypeStruct((DB, n_ch, KV_UNITS * HEAD_DIM), BF16),
        compiler_params=_params(2),
        name="compress_paged",
    )(page_table.reshape(-1), *([cache] * PAGES_PER_STEP), pe_rows, w1cat, b1, w2, b2)


def _nsa_prompt_kernel(q_ref, ks_ref, kw_ref, kcv_ref, gate_ref, ov_ref, e_ref, o_ref, *, n_cmp, n_sel):
    TQ, TK = NSA_TQ, NSA_TK
    R = Q_PER_KV * TQ
    t0 = pl.program_id(1) * TQ
    tpos = t0 + lax.broadcasted_iota(jnp.int32, (TQ, 1), 0)
    tpos4 = jnp.concatenate([tpos] * Q_PER_KV, axis=0)
    NC = kcv_ref.shape[0]
    cidx = lax.broadcasted_iota(jnp.int32, (1, NC), 1)
    m_c = (cidx * CMP_STRIDE + (CMP_BLK - 1) <= tpos4) & (cidx < n_cmp)
    klane = lax.broadcasted_iota(jnp.int32, (1, TK), 1)
    n_sel_tiles = (t0 + TQ + TK - 1) // TK
    w_lo = jnp.maximum(t0 - WINDOW, 0) // TK
    w_hi = (t0 + TQ - 1) // TK + 1

    q4s, o_cs, imps = [], [], []
    for g in range(KV_HEADS):
        q4 = jnp.concatenate([q_ref[:, (Q_PER_KV * g + r) * HEAD_DIM:(Q_PER_KV * g + r + 1) * HEAD_DIM]
                              for r in range(Q_PER_KV)], axis=0)
        p_c = _masked_softmax(_dot_nt(q4, kcv_ref[:, g * HEAD_DIM:(g + 1) * HEAD_DIM]) * ATT_SCALE, m_c)
        o_cs.append(_dot(p_c.astype(BF16), kcv_ref[:, (KV_HEADS + g) * HEAD_DIM:(KV_HEADS + g + 1) * HEAD_DIM]))
        imp4 = _split_dot(p_c, ov_ref[...])
        imp = imp4[0:TQ]
        for r in range(1, Q_PER_KV):
            imp = imp + imp4[r * TQ:(r + 1) * TQ]
        q4s.append(q4)
        imps.append(imp)
    sel_all = _select_blocks(jnp.concatenate(imps, axis=0), jnp.concatenate([tpos] * KV_HEADS, axis=0), n_sel)

    for g in range(KV_HEADS):
        q4, o_c = q4s[g], o_cs[g]
        kcol = slice(g * HEAD_DIM, (g + 1) * HEAD_DIM)
        vcol = slice((KV_HEADS + g) * HEAD_DIM, (KV_HEADS + g + 1) * HEAD_DIM)
        sel = sel_all[g * TQ:(g + 1) * TQ]
        sel4 = jnp.concatenate([sel] * Q_PER_KV, axis=0).astype(BF16)

        def sel_step(kb, carry):
            k0 = pl.multiple_of(kb * TK, TK)
            s = _dot_nt(q4, ks_ref[pl.ds(k0, TK), kcol]) * ATT_SCALE
            in_sel = _dot(sel4, e_ref[kb]) > 0.5
            mask = in_sel & (k0 + klane <= tpos4)
            return _online_update(s, mask, ks_ref[pl.ds(k0, TK), vcol], *carry)

        _, l_s, acc_s = lax.fori_loop(0, n_sel_tiles, sel_step, _online_init(R))
        o_s = _online_finish(l_s, acc_s)

        def win_step(kb, carry):
            k0 = pl.multiple_of(kb * TK, TK)
            s = _dot_nt(q4, kw_ref[pl.ds(k0, TK), kcol]) * ATT_SCALE
            delta = tpos4 - (k0 + klane)
            mask = (delta >= 0) & (delta < WINDOW)
            return _online_update(s, mask, kw_ref[pl.ds(k0, TK), vcol], *carry)

        _, l_w, acc_w = lax.fori_loop(w_lo, w_hi, win_step, _online_init(R))
        o_w = _online_finish(l_w, acc_w)

        for r in range(Q_PER_KV):
            h = Q_PER_KV * g + r
            rows = slice(r * TQ, (r + 1) * TQ)
            o = (gate_ref[:, 3 * h:3 * h + 1] * o_c[rows] + gate_ref[:, 3 * h + 1:3 * h + 2] * o_s[rows]
                 + gate_ref[:, 3 * h + 2:3 * h + 3] * o_w[rows])
            o_ref[:, h * HEAD_DIM:(h + 1) * HEAD_DIM] = o.astype(o_ref.dtype)


def _nsa_prompt(q16, kv16, kcv, gates, B, T):
    TQ, TK = NSA_TQ, NSA_TK
    nq = T // TQ
    NC = kcv.shape[1]
    n_cmp, n_sel = NC - (CMP_R - 1), T // SEL_BLK
    ov = _overlap_matrix(n_cmp, n_sel, NC, LANES)
    kidx = np.arange(T) // SEL_BLK
    e = (np.arange(LANES)[None, :, None] == kidx.reshape(T // TK, 1, TK)).astype(np.float32)
    e = jnp.asarray(e, dtype=BF16)
    kv_w = 2 * KV_COLS
    return pl.pallas_call(
        functools.partial(_nsa_prompt_kernel, n_cmp=n_cmp, n_sel=n_sel),
        grid=(B, nq),
        in_specs=[pl.BlockSpec((TQ, ATT_W), lambda b, i: (b * nq + i, 0)),
                  pl.BlockSpec((T, kv_w), lambda b, i: (b, 1)),
                  pl.BlockSpec((T, kv_w), lambda b, i: (b, 2)),
                  pl.BlockSpec((None, NC, KV_UNITS * HEAD_DIM), lambda b, i: (b, 0, 0)),
                  pl.BlockSpec((TQ, LANES), lambda b, i: (b * nq + i, 0)),
                  pl.BlockSpec(ov.shape, lambda b, i: (0, 0)),
                  pl.BlockSpec(e.shape, lambda b, i: (0, 0, 0))],
        out_specs=pl.BlockSpec((TQ, ATT_W), lambda b, i: (b * nq + i, 0)),
        out_shape=jax.ShapeDtypeStruct((B * T, ATT_W), BF16),
        compiler_params=_params(2),
        name="nsa_prompt",
    )(q16, kv16, kv16, kcv, gates, ov, e)


def _unit_rows(ref, u, n):
    return ref[pl.ds(u, n, stride=KV_UNITS), :]


def _both_heads(ref, kv, n, pad_to=None):
    parts = []
    for g in range(KV_HEADS):
        parts.append(_unit_rows(ref, kv * KV_HEADS + g, n))
        if pad_to is not None and pad_to > n:
            parts.append(jnp.zeros((pad_to - n, HEAD_DIM), F32))
    return jnp.concatenate(parts, axis=0).astype(BF16)


def _log2(n):
    assert n > 0 and n & (n - 1) == 0, n
    return n.bit_length() - 1


def _div(x, n):
    return jnp.right_shift(x, _log2(n))


def _mod(x, n):
    _log2(n)
    return jnp.bitwise_and(x, n - 1)


def _nsa_sample_kernel(pt_ref, *refs, past, n_cmp, n_sel, DS):
    del pt_ref
    pages = refs[:PAGES_PER_STEP]
    (q_ref, kcv_ref, snew_ref, wst_ref, wnew_ref, gate_ref, ov_ref, o_ref,
     sel_ref, oc_ref, m_ref, l_ref, acc_ref) = refs[PAGES_PER_STEP:]
    step = pl.program_id(1)
    R = N_HEADS * DS
    RG = Q_PER_KV * DS
    ridx = lax.broadcasted_iota(jnp.int32, (R, 1), 0)
    tpos = past + _mod(ridx, DS)
    row_g = _div(ridx, RG)
    q = jnp.concatenate([q_ref[:, h * HEAD_DIM:(h + 1) * HEAD_DIM] for h in range(N_HEADS)], axis=0).astype(BF16)

    def same_head(n_lanes, per_head):
        lane = lax.broadcasted_iota(jnp.int32, (1, n_lanes), 1)
        return (_div(lane, per_head) == row_g), _mod(lane, per_head)

    @pl.when(step == 0)
    def _():
        NC = kcv_ref.shape[0]
        kc = jnp.concatenate([kcv_ref[:, g * HEAD_DIM:(g + 1) * HEAD_DIM] for g in range(KV_HEADS)], axis=0)
        vc = jnp.concatenate([kcv_ref[:, (KV_HEADS + g) * HEAD_DIM:(KV_HEADS + g + 1) * HEAD_DIM]
                              for g in range(KV_HEADS)], axis=0)
        same, cidx = same_head(KV_HEADS * NC, NC)
        m_c = same & (cidx * CMP_STRIDE + (CMP_BLK - 1) <= tpos) & (cidx < n_cmp)
        p_c = _masked_softmax(_dot_nt(q, kc) * ATT_SCALE, m_c)
        oc_ref[...] = _dot(p_c.astype(BF16), vc)
        ov2 = jnp.concatenate([ov_ref[...]] * KV_HEADS, axis=0)
        imp_all = _split_dot(p_c, ov2)
        imps = []
        for g in range(KV_HEADS):
            imp = imp_all[g * RG:g * RG + DS]
            for r in range(1, Q_PER_KV):
                imp = imp + imp_all[g * RG + r * DS:g * RG + (r + 1) * DS]
            imps.append(imp)
        sel_all = _select_blocks(jnp.concatenate(imps, axis=0), tpos[0:KV_HEADS * DS], n_sel)
        for g in range(KV_HEADS):
            sel_ref[g * RG:(g + 1) * RG, :] = jnp.concatenate([sel_all[g * DS:(g + 1) * DS]] * Q_PER_KV, axis=0)
        m0, l0, a0 = _online_init(R)
        m_ref[...] = m0
        l_ref[...] = l0
        acc_ref[...] = a0

    sel_b = sel_ref[...].astype(BF16)
    L = sel_ref.shape[1]
    jrow = lax.broadcasted_iota(jnp.int32, (L, 1), 0)
    lane = lax.broadcasted_iota(jnp.int32, (1, PAGES_PER_STEP * KV_HEADS * PAGE_SIZE), 1)
    kpos = (step * PAGES_PER_STEP + _div(lane, KV_HEADS * PAGE_SIZE)) * PAGE_SIZE + _mod(lane, PAGE_SIZE)
    same = _mod(_div(lane, PAGE_SIZE), KV_HEADS) == row_g
    expand = jnp.where(jrow == _div(kpos, SEL_BLK), 1.0, 0.0).astype(BF16)
    mask = same & (_dot(sel_b, expand) > 0.5) & (kpos <= tpos)
    k_all = jnp.concatenate([_both_heads(pg, 0, PAGE_SIZE) for pg in pages], axis=0)
    v_all = jnp.concatenate([_both_heads(pg, 1, PAGE_SIZE) for pg in pages], axis=0)
    s = _dot_nt(q, k_all) * ATT_SCALE
    m_ref[...], l_ref[...], acc_ref[...] = _online_update(s, mask, v_all, m_ref[...], l_ref[...], acc_ref[...])

    @pl.when(step == pl.num_programs(1) - 1)
    def _():
        same_n, tn = same_head(KV_HEADS * LANES, LANES)
        same_n = same_n & (tn < DS)
        npos = past + tn
        expand = jnp.where(jrow == _div(npos, SEL_BLK), 1.0, 0.0).astype(BF16)
        mask = same_n & (_dot(sel_b, expand) > 0.5) & (npos <= tpos)
        s = _dot_nt(q, _both_heads(snew_ref, 0, DS, LANES)) * ATT_SCALE
        _, l_s, acc_s = _online_update(s, mask, _both_heads(snew_ref, 1, DS, LANES),
                                       m_ref[...], l_ref[...], acc_ref[...])
        o_s = _online_finish(l_s, acc_s)

        same_w, win = same_head(KV_HEADS * WINDOW, WINDOW)
        wpos = past - WINDOW + win
        delta = tpos - wpos
        mask = same_w & (delta >= 0) & (delta < WINDOW) & (wpos >= 0)
        s = _dot_nt(q, _both_heads(wst_ref, 0, WINDOW)) * ATT_SCALE
        carry_w = _online_update(s, mask, _both_heads(wst_ref, 1, WINDOW), *_online_init(R))
        delta = tpos - npos
        mask = same_n & (delta >= 0) & (delta < WINDOW)
        s = _dot_nt(q, _both_heads(wnew_ref, 0, DS, LANES)) * ATT_SCALE
        _, l_w, acc_w = _online_update(s, mask, _both_heads(wnew_ref, 1, DS, LANES), *carry_w)
        o_w = _online_finish(l_w, acc_w)

        o_c = oc_ref[...]
        for h in range(N_HEADS):
            rows = slice(h * DS, (h + 1) * DS)
            o = (gate_ref[:, 3 * h:3 * h + 1] * o_c[rows] + gate_ref[:, 3 * h + 1:3 * h + 2] * o_s[rows]
                 + gate_ref[:, 3 * h + 2:3 * h + 3] * o_w[rows])
            o_ref[:, h * HEAD_DIM:(h + 1) * HEAD_DIM] = o


def _nsa_sample(q, kcv, sel_new, win_state, win_new, gates, cache_sel, layer, page_table):
    DB, DS, _ = q.shape
    n_pages = page_table.shape[1]
    past = n_pages * PAGE_SIZE
    assert n_pages % PAGES_PER_STEP == 0 and DS % 8 == 0 and SEL_BLK % DS == 0
    n_steps = n_pages // PAGES_PER_STEP
    NC = kcv.shape[1]
    n_cmp = (past + DS) // CMP_STRIDE - (CMP_R - 1)
    assert NC == n_cmp + (CMP_R - 1)
    n_sel = -(-(past + DS) // SEL_BLK)
    L = -(-n_sel // LANES) * LANES
    ov = _overlap_matrix(n_cmp, n_sel, NC, L)
    R = N_HEADS * DS

    def page_spec(k):
        return pl.BlockSpec((None, None, PAGE_SIZE * KV_UNITS, HEAD_DIM),
                            lambda b, s, pt: (layer, pt[b * n_pages + s * PAGES_PER_STEP + k], 0, 0))

    per_b = lambda shape: pl.BlockSpec((None,) + shape, lambda b, s, pt: (b,) + (0,) * len(shape))
    grid_spec = pltpu.PrefetchScalarGridSpec(
        num_scalar_prefetch=1,
        grid=(DB, n_steps),
        in_specs=[page_spec(k) for k in range(PAGES_PER_STEP)] + [
            per_b((DS, ATT_W)),
            per_b((NC, KV_UNITS * HEAD_DIM)),
            per_b((DS * KV_UNITS, HEAD_DIM)),
            pl.BlockSpec((None, None, WINDOW * KV_UNITS, HEAD_DIM), lambda b, s, pt: (layer, b, 0, 0)),
            per_b((DS * KV_UNITS, HEAD_DIM)),
            per_b((DS, LANES)),
            pl.BlockSpec(ov.shape, lambda b, s, pt: (0, 0))],
        out_specs=per_b((DS, ATT_W)),
        scratch_shapes=[pltpu.VMEM((R, L), F32), pltpu.VMEM((R, HEAD_DIM), F32), pltpu.VMEM((R, 1), F32),
                        pltpu.VMEM((R, 1), F32), pltpu.VMEM((R, HEAD_DIM), F32)],
    )
    return pl.pallas_call(
        functools.partial(_nsa_sample_kernel, past=past, n_cmp=n_cmp, n_sel=n_sel, DS=DS),
        grid_spec=grid_spec,
        out_shape=jax.ShapeDtypeStruct((DB, DS, ATT_W), F32),
        compiler_params=_params(2),
        name="nsa_sample",
    )(page_table.reshape(-1), *([cache_sel] * PAGES_PER_STEP), q, kcv, sel_new, win_state, win_new, gates, ov)


def _moe_up_kernel(be_ref, bv_ref, x_ref, w_ref, b_ref, pick_ref, o_ref, wbf_ref):
    i = pl.program_id(1)
    prev_e = be_ref[jnp.maximum(i - 1, 0)]

    @pl.when((i == 0) | (be_ref[i] != prev_e))
    def _():
        wbf_ref[...] = w_ref[...].astype(BF16)

    @pl.when(bv_ref[i] != 0)
    def _():
        h = _dot(x_ref[...], wbf_ref[...]) + b_ref[...]
        g = jnp.minimum(h, SWIGLU_LIMIT)
        lin = jnp.clip(h, -SWIGLU_LIMIT, SWIGLU_LIMIT) + 1.0
        t = g * jax.nn.sigmoid(SWIGLU_ALPHA * g) * pltpu.roll(lin, h.shape[1] - 1, 1)
        o_ref[...] = _dot(t.astype(BF16), pick_ref[...]).astype(o_ref.dtype)

    @pl.when(bv_ref[i] == 0)
    def _():
        o_ref[...] = jnp.zeros_like(o_ref)


def _moe_up(block_e, block_valid, xs, w_up, b_up, layer, tn):
    rows, D = xs.shape
    F = w_up.shape[3] // 2
    nb = rows // MOE_TM
    pick = jnp.asarray(np.arange(2 * tn)[:, None] == 2 * np.arange(tn)[None, :], dtype=BF16)
    grid_spec = pltpu.PrefetchScalarGridSpec(
        num_scalar_prefetch=2,
        grid=(F // tn, nb),
        in_specs=[pl.BlockSpec((MOE_TM, D), lambda j, i, be, bv: (i, 0)),
                  pl.BlockSpec((None, None, D, 2 * tn), lambda j, i, be, bv: (layer, be[i], 0, j)),
                  pl.BlockSpec((None, None, 1, 2 * tn), lambda j, i, be, bv: (layer, be[i], 0, j)),
                  pl.BlockSpec((2 * tn, tn), lambda j, i, be, bv: (0, 0))],
        out_specs=pl.BlockSpec((MOE_TM, tn), lambda j, i, be, bv: (i, j)),
        scratch_shapes=[pltpu.VMEM((D, 2 * tn), BF16)],
    )
    return pl.pallas_call(
        _moe_up_kernel,
        grid_spec=grid_spec,
        out_shape=jax.ShapeDtypeStruct((rows, F), BF16),
        compiler_params=_params(2),
        name="moe_up",
    )(block_e, block_valid, xs, w_up, b_up, pick)


def _moe_down_kernel(be_ref, bv_ref, a_ref, w_ref, b_ref, rw_ref, o_ref, wbf_ref):
    i = pl.program_id(1)
    prev_e = be_ref[jnp.maximum(i - 1, 0)]

    @pl.when((i == 0) | (be_ref[i] != prev_e))
    def _():
        wbf_ref[...] = w_ref[...].astype(BF16)

    @pl.when(bv_ref[i] != 0)
    def _():
        h = _dot(a_ref[...], wbf_ref[...]) + b_ref[...]
        o_ref[...] = h * rw_ref[...]

    @pl.when(bv_ref[i] == 0)
    def _():
        o_ref[...] = jnp.zeros_like(o_ref)


def _moe_down(block_e, block_valid, act, w_down, b_down, row_w, layer, tn):
    rows, F = act.shape
    D = w_down.shape[3]
    nb = rows // MOE_TM
    grid_spec = pltpu.PrefetchScalarGridSpec(
        num_scalar_prefetch=2,
        grid=(D // tn, nb),
        in_specs=[pl.BlockSpec((MOE_TM, F), lambda j, i, be, bv: (i, 0)),
                  pl.BlockSpec((None, None, F, tn), lambda j, i, be, bv: (layer, be[i], 0, j)),
                  pl.BlockSpec((None, None, 1, tn), lambda j, i, be, bv: (layer, be[i], 0, j)),
                  pl.BlockSpec((MOE_TM, 1), lambda j, i, be, bv: (i, 0))],
        out_specs=pl.BlockSpec((MOE_TM, tn), lambda j, i, be, bv: (i, j)),
        scratch_shapes=[pltpu.VMEM((F, tn), BF16)],
    )
    return pl.pallas_call(
        _moe_down_kernel,
        grid_spec=grid_spec,
        out_shape=jax.ShapeDtypeStruct((rows, D), F32),
        compiler_params=_params(2),
        name="moe_down",
    )(block_e, block_valid, act, w_down, b_down, row_w)


def _moe(x, router_w, router_b, w_up, b_up, w_down, b_down, layer):
    N, D = x.shape
    logits = jnp.dot(x, router_w, precision=HI) + router_b
    top_v, top_e = lax.top_k(logits, TOP_K)
    gate = jax.nn.softmax(top_v, axis=-1)
    M = N * TOP_K
    flat_e = top_e.reshape(M)
    order = jnp.argsort(flat_e).astype(jnp.int32)
    counts = jnp.sum(flat_e[:, None] == jnp.arange(N_EXPERTS, dtype=flat_e.dtype)[None, :], axis=0).astype(jnp.int32)
    padded = (counts + MOE_TM - 1) // MOE_TM * MOE_TM
    start = jnp.cumsum(counts) - counts
    pend = jnp.cumsum(padded)
    pstart = pend - padded
    nb = -(-(M + N_EXPERTS * (MOE_TM - 1)) // MOE_TM)
    rows = nb * MOE_TM
    blk0 = jnp.arange(nb, dtype=jnp.int32) * MOE_TM
    block_e = jnp.minimum(jnp.sum(pend[None, :] <= blk0[:, None], axis=1), N_EXPERTS - 1).astype(jnp.int32)
    block_valid = (blk0 < pend[-1]).astype(jnp.int32)
    p = jnp.arange(rows, dtype=jnp.int32)
    e_p = block_e[p // MOE_TM]
    q = p - pstart[e_p]
    live = q < counts[e_p]
    src = order[jnp.clip(start[e_p] + q, 0, M - 1)]
    row_tok = jnp.where(live, src // TOP_K, N).astype(jnp.int32)
    row_w = jnp.where(live, gate.reshape(M)[src], 0.0)
    slot = pstart[flat_e] + jnp.argsort(order).astype(jnp.int32) - start[flat_e]

    x_pad = jnp.concatenate([x.astype(BF16), jnp.zeros((1, D), BF16)], axis=0)
    xs = x_pad[row_tok]
    act = _moe_up(block_e, block_valid, xs, w_up, b_up.reshape(DEPTH, N_EXPERTS, 1, 2 * D_FF), layer, tn=512)
    out = _moe_down(block_e, block_valid, act, w_down, b_down.reshape(DEPTH, N_EXPERTS, 1, D), row_w[:, None], layer, tn=1024)
    return out[slot.reshape(N, TOP_K).T.reshape(M)]


def _residual_ln_kernel(*refs, n_add):
    adds, (x_ref, g_ref, b_ref, o_ref) = refs[:n_add], refs[n_add:]
    y = adds[0][...]
    for a in adds[1:]:
        y = y + a[...]
    z = DN_ALPHA * x_ref[...] + y
    mu = jnp.mean(z, axis=-1, keepdims=True)
    var = jnp.mean(jnp.square(z - mu), axis=-1, keepdims=True)
    o_ref[...] = (z - mu) * lax.rsqrt(var + LN_EPS) * g_ref[...] + b_ref[...]


def _residual_ln(stacked, n_add, x, g, b):
    N, D = x.shape
    tm = 256
    nt = N // tm
    assert N % tm == 0 and stacked.shape == (n_add * N, D)
    return pl.pallas_call(
        functools.partial(_residual_ln_kernel, n_add=n_add),
        grid=(nt,),
        in_specs=[pl.BlockSpec((tm, D), functools.partial(lambda i, k: (k * nt + i, 0), k=k)) for k in range(n_add)] + [
            pl.BlockSpec((tm, D), lambda i: (i, 0)),
            pl.BlockSpec((1, D), lambda i: (0, 0)),
            pl.BlockSpec((1, D), lambda i: (0, 0))],
        out_specs=pl.BlockSpec((tm, D), lambda i: (i, 0)),
        out_shape=jax.ShapeDtypeStruct((N, D), F32),
        compiler_params=_params(1),
        name="residual_ln",
    )(*([stacked] * n_add), x, g.reshape(1, D), b.reshape(1, D))


def _conv_body(u, hu, w_ref, b_ref, g_ref, beta_ref, o_ref, tail_ref, s_ref, y_ref):
    TT = u.shape[0]
    s_ref[0:CONV_HALO, :] = hu
    s_ref[CONV_HALO:CONV_HALO + TT, :] = u
    off = CONV_HALO - (CONV_W - 1)
    for cc in range(C_CONV // LANES):
        cols = slice(cc * LANES, (cc + 1) * LANES)
        acc = jnp.broadcast_to(b_ref[:, cols], (TT, LANES))
        for j in range(CONV_W):
            acc = acc + s_ref[pl.ds(j + off, TT), cols] * w_ref[j:j + 1, cols]
        y_ref[:, cols] = acc
    y = y_ref[...]
    mu = jnp.mean(y, axis=-1, keepdims=True)
    var = jnp.mean(jnp.square(y - mu), axis=-1, keepdims=True)
    yn = (y - mu) * lax.rsqrt(var + LN_EPS) * g_ref[...] + beta_ref[...]
    o_ref[...] = (yn * jax.nn.sigmoid(yn)).astype(o_ref.dtype)
    tail_ref[...] = s_ref[TT:TT + CONV_HALO, :]


def _glu(ref):
    return ref[:, :C_CONV] * jax.nn.sigmoid(ref[:, C_CONV:])


def _conv_prompt_kernel(main_ref, halo_ref, *rest):
    hu = jnp.where(pl.program_id(1) > 0, _glu(halo_ref), 0.0)
    _conv_body(_glu(main_ref), hu, *rest)


def _conv_sample_kernel(main_ref, state_ref, *rest):
    _conv_body(_glu(main_ref), state_ref[...], *rest)


def _conv_call(kern, grid, in_specs, out_rows_spec, n_seq, TT, n_rows, out_dtype, name, args):
    C = C_CONV
    const = lambda shape: pl.BlockSpec(shape, lambda b, i: (0, 0))
    return pl.pallas_call(
        kern,
        grid=grid,
        in_specs=in_specs + [const((CONV_W, C)), const((1, C)), const((1, C)), const((1, C))],
        out_specs=[out_rows_spec, pl.BlockSpec((None, CONV_HALO, C), lambda b, i: (b, 0, 0))],
        out_shape=[jax.ShapeDtypeStruct((n_rows, C), out_dtype), jax.ShapeDtypeStruct((n_seq, CONV_HALO, C), F32)],
        scratch_shapes=[pltpu.VMEM((CONV_HALO + TT, C), F32), pltpu.VMEM((TT, C), F32)],
        compiler_params=_params(2),
        name=name,
    )(*args)


def _conv_prompt(cin, B, T, conv_w, conv_b, ln_g, ln_b):
    TT = 128
    nt, hb = T // TT, TT // CONV_HALO
    C = C_CONV
    in_specs = [pl.BlockSpec((TT, 2 * C), lambda b, i: (b * nt + i, 0)),
                pl.BlockSpec((CONV_HALO, 2 * C), lambda b, i: (jnp.maximum((b * nt + i) * hb - 1, 0), 0))]
    return _conv_call(_conv_prompt_kernel, (B, nt), in_specs, pl.BlockSpec((TT, C), lambda b, i: (b * nt + i, 0)),
                      B, TT, B * T, BF16, "conv_prompt",
                      (cin, cin, conv_w, conv_b.reshape(1, C), ln_g.reshape(1, C), ln_b.reshape(1, C)))


def _conv_sample(cin, row0, DB, DS, state, conv_w, conv_b, ln_g, ln_b):
    C = C_CONV
    assert row0 % DS == 0
    in_specs = [pl.BlockSpec((DS, 2 * C), lambda b, i: (row0 // DS + b, 0)),
                pl.BlockSpec((None, CONV_HALO, C), lambda b, i: (b, 0, 0))]
    return _conv_call(_conv_sample_kernel, (DB, 1), in_specs, pl.BlockSpec((DS, C), lambda b, i: (b, 0)),
                      DB, DS, DB * DS, F32, "conv_sample",
                      (cin, state, conv_w, conv_b.reshape(1, C), ln_g.reshape(1, C), ln_b.reshape(1, C)))


def _rope_tables(pos):
    inv = ROPE_THETA ** (-jnp.arange(0, HEAD_DIM, 2, dtype=F32) / HEAD_DIM)
    ang = pos.astype(F32)[:, None] * inv[None, :]
    cos, sin = jnp.cos(ang), jnp.sin(ang)
    return jnp.concatenate([cos, cos], axis=-1), jnp.concatenate([-sin, sin], axis=-1)


def kernel(x_prompt, x_sample, cache_cmp_kv, cache_sel_kv, state_win_kv, state_conv, page_table,
           w_in, cmp_pe, cmp_w1, cmp_b1, cmp_w2, cmp_b2, conv_w, conv_b, conv_ln_g, conv_ln_b,
           w_out, ln1_g, ln1_b, router_w, router_b, exp_w_up, exp_b_up, exp_w_down, exp_b_down,
           ln2_g, ln2_b):
    B, T, _ = x_prompt.shape
    DB, DS, _ = x_sample.shape
    n_pages = page_table.shape[1]
    n_pool = cache_cmp_kv.shape[1]
    past = n_pages * PAGE_SIZE
    NP, NS = B * T, DB * DS
    pos = jnp.concatenate([jnp.tile(jnp.arange(T, dtype=jnp.int32), B),
                           jnp.tile(past + jnp.arange(DS, dtype=jnp.int32), DB)])
    cos_t, sin_t = _rope_tables(pos)
    cache_cmp = cache_cmp_kv.reshape(DEPTH, n_pool, PAGE_SIZE * KV_UNITS, HEAD_DIM)
    cache_sel = cache_sel_kv.reshape(DEPTH, n_pool, PAGE_SIZE * KV_UNITS, HEAD_DIM)
    win_state = state_win_kv.reshape(DEPTH, DB, WINDOW * KV_UNITS, HEAD_DIM)
    kv_shape = (2, KV_HEADS, HEAD_DIM)
    conv_state = jnp.pad(state_conv, ((0, 0), (0, 0), (CONV_HALO - (CONV_W - 1), 0), (0, 0)))
    x = jnp.concatenate([x_prompt.reshape(NP, D_MODEL), x_sample.reshape(NS, D_MODEL)], axis=0)
    outs = [[] for _ in range(8)]
    p4, s4 = NP * KV_UNITS, NS * KV_UNITS
    for l in range(DEPTH):
        xb = x.astype(BF16)
        w_gate = jnp.pad(w_in[l][:, QKV_COLS:QKV_COLS + GATE_COLS], ((0, 0), (0, LANES - GATE_COLS)))
        q16 = _q_proj(xb, w_in, l, cos_t, sin_t)
        kvi, kv16 = _kv_proj(xb, w_in, l, cos_t, sin_t)
        cin = _matmul(xb, w_in[l][:, QKV_COLS + GATE_COLS:], tm=ROW_TM, tn=512)
        gates = _matmul(xb, w_gate, tm=ROW_TM, tn=LANES, sigmoid=True)
        cw = _compress_weights(cmp_pe[l], cmp_w1[l], cmp_b1[l], cmp_w2[l], cmp_b2[l])
        kv_s = kvi[:, p4:p4 + s4].reshape(3, DB, DS * KV_UNITS, HEAD_DIM)

        kcv_p = _compress_prompt(kvi, B, T, cw)
        att_p = _nsa_prompt(q16, kv16, kcv_p, gates, B, T)
        conv_p, tail_p = _conv_prompt(cin, B, T, conv_w[l], conv_b[l], conv_ln_g[l], conv_ln_b[l])
        assert (past + DS) // CMP_STRIDE == past // CMP_STRIDE
        kcv_s = _compress_paged(cache_cmp, l, page_table, cw)
        att_s = _nsa_sample(q16[NP:].astype(F32).reshape(DB, DS, ATT_W), kcv_s, kv_s[1], win_state, kv_s[2],
                            gates[NP:].reshape(DB, DS, LANES), cache_sel, l, page_table)
        conv_s, tail_s = _conv_sample(cin, NP, DB, DS, conv_state[l], conv_w[l], conv_b[l], conv_ln_g[l], conv_ln_b[l])

        kv_p = kvi[:, :p4].reshape((3, B, T) + kv_shape)
        outs[0].append(kv_p[0])
        outs[1].append(kv_p[1])
        outs[2].append(kv_p[2][:, -WINDOW:])
        outs[3].append(tail_p[:, CONV_HALO - (CONV_W - 1):])
        outs[4].append(kv_s[0].reshape((DB, DS) + kv_shape))
        outs[5].append(kv_s[1].reshape((DB, DS) + kv_shape))
        outs[6].append(jnp.concatenate([state_win_kv[l], kv_s[2].reshape((DB, DS) + kv_shape)], axis=1)[:, -WINDOW:])
        outs[7].append(tail_s[:, CONV_HALO - (CONV_W - 1):])

        att = jnp.concatenate([att_p, att_s.reshape(NS, ATT_W).astype(BF16)], axis=0)
        conv_out = jnp.concatenate([conv_p, conv_s.astype(BF16)], axis=0)
        mix = _matmul(jnp.concatenate([att, conv_out], axis=1), w_out, tm=ROW_TM, tn=512, layer=l)
        x = _residual_ln(mix, 1, x, ln1_g[l], ln1_b[l])
        y4 = _moe(x, router_w[l], router_b[l], exp_w_up, exp_b_up, exp_w_down, exp_b_down, l)
        x = _residual_ln(y4, TOP_K, x, ln2_g[l], ln2_b[l])

    return (x[:NP].reshape(B, T, D_MODEL), x[NP:].reshape(DB, DS, D_MODEL)) + tuple(jnp.stack(o) for o in outs)
```

```python
import functools

import jax
import jax.numpy as jnp
import numpy as np
from jax import lax
from jax.experimental import pallas as pl
from jax.experimental.pallas import tpu as pltpu

D_MODEL = 2048
DEPTH = 2
PAGE_SIZE = 128
N_HEADS = 8
HEAD_DIM = 128
KV_HEADS = 2
Q_PER_KV = N_HEADS // KV_HEADS
ATT_W = N_HEADS * HEAD_DIM
KV_COLS = KV_HEADS * HEAD_DIM
KV_UNITS = 2 * KV_HEADS
CMP_BLK = 32
CMP_STRIDE = 16
CMP_R = CMP_BLK // CMP_STRIDE
SEL_BLK = 64
N_SEL_TOP = 16
WINDOW = 512
FORCE_BONUS = 1000.0
NEG = -1e30
BELOW_NEG = -3e38
ROPE_THETA = 10000.0
ATT_SCALE = HEAD_DIM ** -0.5
C_CONV = D_MODEL - ATT_W
CONV_W = 31
QKV_COLS = ATT_W + 6 * KV_COLS
GATE_COLS = 3 * N_HEADS
N_EXPERTS = 32
TOP_K = 4
D_FF = 2048
SWIGLU_LIMIT = 7.0
SWIGLU_ALPHA = 1.702
LN_EPS = 1e-5
DN_ALPHA = (2 * DEPTH) ** 0.25

LANES = 128
VMEM_LIMIT = 56 * 1024 * 1024
MOE_TM = 256
ROW_TM = 768
NSA_TQ = 128
NSA_TK = 256
NSA_ROW_SPLIT = 2
PAGES_PER_STEP = 8
CONV_HALO = 32
HI = lax.Precision.HIGHEST
F32 = jnp.float32
BF16 = jnp.bfloat16


def _params(n_axes):
    return pltpu.CompilerParams(dimension_semantics=("arbitrary",) * n_axes, vmem_limit_bytes=VMEM_LIMIT)


def _dot(a, b):
    return jnp.dot(a, b, preferred_element_type=F32)


def _dot_nt(a, b):
    return lax.dot_general(a, b, (((1,), (1,)), ((), ())), preferred_element_type=F32)


def _split_dot(p, w):
    hi = p.astype(BF16)
    r1 = p - hi.astype(F32)
    mid = r1.astype(BF16)
    lo = (r1 - mid.astype(F32)).astype(BF16)
    return _dot(hi, w) + _dot(mid, w) + _dot(lo, w)


def _mm_kernel(a_ref, w_ref, o_ref, wbf_ref, *, sigmoid):
    @pl.when(pl.program_id(1) == 0)
    def _():
        wbf_ref[...] = w_ref[...].astype(BF16)

    h = _dot(a_ref[...], wbf_ref[...])
    o_ref[...] = (jax.nn.sigmoid(h) if sigmoid else h).astype(o_ref.dtype)


def _w_spec(w, layer, K, tn, col0=0):
    if layer is None:
        return pl.BlockSpec((K, tn), lambda j, i: (0, col0 + j))
    assert w.ndim == 3
    return pl.BlockSpec((None, K, tn), lambda j, i: (layer, 0, col0 + j))


def _matmul(a, w, tm, tn, sigmoid=False, out_dtype=F32, layer=None):
    M, K = a.shape
    N = w.shape[-1]
    assert M % tm == 0 and N % tn == 0
    return pl.pallas_call(
        functools.partial(_mm_kernel, sigmoid=sigmoid),
        grid=(N // tn, M // tm),
        in_specs=[pl.BlockSpec((tm, K), lambda j, i: (i, 0)),
                  _w_spec(w, layer, K, tn)],
        out_specs=pl.BlockSpec((tm, tn), lambda j, i: (i, j)),
        out_shape=jax.ShapeDtypeStruct((M, N), out_dtype),
        scratch_shapes=[pltpu.VMEM((K, tn), BF16)],
        compiler_params=_params(2),
        name="dense_matmul",
    )(a, w)


def _rope(x, cos, sin):
    return x * cos + pltpu.roll(x, HEAD_DIM // 2, 1) * sin


def _q_kernel(a_ref, w_ref, cos_ref, sin_ref, o16_ref, wbf_ref):
    @pl.when(pl.program_id(1) == 0)
    def _():
        wbf_ref[...] = w_ref[...].astype(BF16)

    h = _dot(a_ref[...], wbf_ref[...])
    for u in range(4):
        cols = slice(u * HEAD_DIM, (u + 1) * HEAD_DIM)
        o16_ref[:, cols] = _rope(h[:, cols], cos_ref[...], sin_ref[...]).astype(BF16)


def _kv_kernel(a_ref, w_ref, cos_ref, sin_ref, o32_ref, o16_ref, wbf_ref):
    @pl.when(pl.program_id(1) == 0)
    def _():
        wbf_ref[...] = w_ref[...].astype(BF16)

    h = _dot(a_ref[...], wbf_ref[...])
    tm = h.shape[0]
    for u in range(KV_UNITS):
        cols = slice(u * HEAD_DIM, (u + 1) * HEAD_DIM)
        y = _rope(h[:, cols], cos_ref[...], sin_ref[...]) if u < KV_HEADS else h[:, cols]
        o32_ref[pl.ds(u, tm, stride=KV_UNITS), :] = y
        o16_ref[:, cols] = y.astype(BF16)


def _proj_specs(w, layer, K, tn, col0):
    return [pl.BlockSpec((ROW_TM, K), lambda j, i: (i, 0)),
            _w_spec(w, layer, K, tn, col0),
            pl.BlockSpec((ROW_TM, HEAD_DIM), lambda j, i: (i, 0)),
            pl.BlockSpec((ROW_TM, HEAD_DIM), lambda j, i: (i, 0))]


def _q_proj(a, w_in, layer, cos_t, sin_t):
    M, K = a.shape
    N = ATT_W
    tn = 4 * HEAD_DIM
    return pl.pallas_call(
        _q_kernel,
        grid=(N // tn, M // ROW_TM),
        in_specs=_proj_specs(w_in, layer, K, tn, 0),
        out_specs=pl.BlockSpec((ROW_TM, tn), lambda j, i: (i, j)),
        out_shape=jax.ShapeDtypeStruct((M, N), BF16),
        scratch_shapes=[pltpu.VMEM((K, tn), BF16)],
        compiler_params=_params(2),
        name="q_proj",
    )(a, w_in, cos_t, sin_t)


def _kv_proj(a, w_in, layer, cos_t, sin_t):
    M, K = a.shape
    N = 6 * KV_COLS
    tn = 2 * KV_COLS
    return pl.pallas_call(
        _kv_kernel,
        grid=(N // tn, M // ROW_TM),
        in_specs=_proj_specs(w_in, layer, K, tn, ATT_W // tn),
        out_specs=[pl.BlockSpec((None, KV_UNITS * ROW_TM, HEAD_DIM), lambda j, i: (j, i, 0)),
                   pl.BlockSpec((ROW_TM, tn), lambda j, i: (i, j))],
        out_shape=[jax.ShapeDtypeStruct((N // tn, KV_UNITS * M, HEAD_DIM), F32), jax.ShapeDtypeStruct((M, N), BF16)],
        scratch_shapes=[pltpu.VMEM((K, tn), BF16)],
        compiler_params=_params(2),
        name="kv_proj",
    )(a, w_in, cos_t, sin_t)


def _masked_softmax(s, mask):
    s = jnp.where(mask, s, NEG)
    m = jnp.max(s, axis=-1, keepdims=True)
    e = jnp.where(mask, jnp.exp(s - m), 0.0)
    return e / jnp.maximum(jnp.sum(e, axis=-1, keepdims=True), 1e-30)


def _online_update(s, mask, v, m, l, acc):
    s = jnp.where(mask, s, NEG)
    m_new = jnp.maximum(m, jnp.max(s, axis=-1, keepdims=True))
    alpha = jnp.exp(m - m_new)
    p = jnp.where(mask, jnp.exp(s - m_new), 0.0)
    l = alpha * l + jnp.sum(p, axis=-1, keepdims=True)
    acc = alpha * acc + _dot(p.astype(BF16), v)
    return m_new, l, acc


def _online_init(rows):
    return (jnp.full((rows, 1), NEG, F32), jnp.zeros((rows, 1), F32), jnp.zeros((rows, HEAD_DIM), F32))


def _online_finish(l, acc):
    return acc / jnp.maximum(l, 1e-30)


def _select_blocks(imp, tpos, n_sel):
    L = imp.shape[1]
    jidx = lax.broadcasted_iota(jnp.int32, (1, L), 1)
    jf = jidx.astype(F32)
    cur = _div(tpos, SEL_BLK)
    valid = (jidx * SEL_BLK <= tpos) & (jidx < n_sel)
    forced = (jidx == 0) | (jidx == cur) | (jidx == cur - 1)
    score = jnp.where(valid, imp + jnp.where(forced, FORCE_BONUS, 0.0), NEG)
    score = jnp.where(jidx < n_sel, score, BELOW_NEG)
    picked = jnp.zeros(imp.shape, F32)
    for _ in range(min(N_SEL_TOP, n_sel)):
        m = jnp.max(score, axis=-1, keepdims=True)
        first = jnp.min(jnp.where(score == m, jf, float(L)), axis=-1, keepdims=True)
        pick = jf == first
        picked = jnp.where(pick, 1.0, picked)
        score = jnp.where(pick, BELOW_NEG, score)
    return jnp.where(valid, picked, 0.0)


def _overlap_matrix(n_cmp, n_sel, rows, cols):
    i = np.arange(rows)[:, None]
    j = np.arange(cols)[None, :]
    ov = (i * CMP_STRIDE < (j + 1) * SEL_BLK) & (i * CMP_STRIDE + CMP_BLK > j * SEL_BLK) & (i < n_cmp) & (j < n_sel)
    return jnp.asarray(ov.astype(np.float32), dtype=BF16)


def _compress_body(step, n_steps, row_refs, pe_ref, w1_ref, b1_ref, w2_ref, b2_ref, o_ref, x_ref, n_ch):
    cpr = row_refs[0].shape[0] // (KV_UNITS * CMP_STRIDE)
    cps = cpr * len(row_refs)
    for k, ref in enumerate(row_refs):
        base = pl.multiple_of(step * cps + k * cpr, 8)
        for u in range(KV_UNITS):
            for s in range(CMP_STRIDE):
                col = (u * CMP_STRIDE + s) * HEAD_DIM
                x_ref[pl.ds(base, cpr), col:col + HEAD_DIM] = ref[pl.ds(KV_UNITS * s + u, cpr, stride=KV_UNITS * CMP_STRIDE), :]

    @pl.when(step == n_steps - 1)
    def _():
        x_ref[n_ch:n_ch + 16, :] = pe_ref[...]
        width = CMP_STRIDE * HEAD_DIM
        ridx = lax.broadcasted_iota(jnp.int32, (n_ch, 1), 0)
        for u in range(KV_UNITS):
            kk = u // KV_HEADS
            acc = _dot(x_ref[:, u * width:(u + 1) * width].astype(BF16), w1_ref[kk])
            pe_term = acc[n_ch:n_ch + 1, :HEAD_DIM] + acc[n_ch + 1:n_ch + 2, HEAD_DIM:]
            z = acc[:n_ch, :HEAD_DIM] + pltpu.roll(acc[:n_ch, HEAD_DIM:], n_ch - 1, 0) + (b1_ref[kk] + pe_term)
            c = _dot(jax.nn.gelu(z).astype(BF16), w2_ref[kk]) + b2_ref[kk]
            c = jnp.where(ridx < n_ch - (CMP_R - 1), c, 0.0)
            o_ref[:, u * HEAD_DIM:(u + 1) * HEAD_DIM] = c.astype(o_ref.dtype)


def _compress_prompt_kernel(rows_ref, pe_ref, w1_ref, b1_ref, w2_ref, b2_ref, o_ref, x_ref, *, n_ch):
    _compress_body(pl.program_id(1), pl.num_programs(1), [rows_ref], pe_ref, w1_ref, b1_ref, w2_ref, b2_ref,
                   o_ref, x_ref, n_ch)


def _compress_paged_kernel(pt_ref, *refs, n_ch):
    del pt_ref
    pages = list(refs[:PAGES_PER_STEP])
    pe_ref, w1_ref, b1_ref, w2_ref, b2_ref, o_ref, x_ref = refs[PAGES_PER_STEP:]
    _compress_body(pl.program_id(1), pl.num_programs(1), pages, pe_ref, w1_ref, b1_ref, w2_ref, b2_ref,
                   o_ref, x_ref, n_ch)


def _compress_weights(pe, w1, b1, w2, b2):
    hid = w1.shape[-1]
    w1r = w1.reshape(2, CMP_R, CMP_STRIDE, HEAD_DIM, hid)
    w1cat = w1r.transpose(0, 2, 3, 1, 4).reshape(2, CMP_STRIDE * HEAD_DIM, CMP_R * hid).astype(BF16)
    per = pe.reshape(2, CMP_R, CMP_STRIDE * HEAD_DIM)
    per = jnp.broadcast_to(per[:, None], (2, KV_HEADS, CMP_R, CMP_STRIDE * HEAD_DIM))
    per = per.transpose(2, 0, 1, 3).reshape(CMP_R, KV_UNITS * CMP_STRIDE * HEAD_DIM)
    pe_rows = jnp.concatenate([per, jnp.zeros((16 - CMP_R, per.shape[1]), F32)], axis=0)
    return pe_rows, w1cat, b1.reshape(2, 1, hid), w2.astype(BF16), b2.reshape(2, 1, HEAD_DIM)


def _compress_prompt(kvi, B, T, cw):
    R4 = T * KV_UNITS
    n_ch = R4 // (KV_UNITS * CMP_STRIDE)
    n_steps = 2
    pe_rows, w1cat, b1, w2, b2 = cw
    xw = KV_UNITS * CMP_STRIDE * HEAD_DIM
    return pl.pallas_call(
        functools.partial(_compress_prompt_kernel, n_ch=n_ch),
        grid=(B, n_steps),
        in_specs=[pl.BlockSpec((None, R4 // n_steps, HEAD_DIM), lambda b, s: (0, b * n_steps + s, 0)),
                  pl.BlockSpec(pe_rows.shape, lambda b, s: (0, 0)),
                  pl.BlockSpec(w1cat.shape, lambda b, s: (0, 0, 0)),
                  pl.BlockSpec(b1.shape, lambda b, s: (0, 0, 0)),
                  pl.BlockSpec(w2.shape, lambda b, s: (0, 0, 0)),
                  pl.BlockSpec(b2.shape, lambda b, s: (0, 0, 0))],
        out_specs=pl.BlockSpec((None, n_ch, KV_UNITS * HEAD_DIM), lambda b, s: (b, 0, 0)),
        out_shape=jax.ShapeDtypeStruct((B, n_ch, KV_UNITS * HEAD_DIM), BF16),
        scratch_shapes=[pltpu.VMEM((n_ch + 16, xw), F32)],
        compiler_params=_params(2),
        name="compress_prompt",
    )(kvi, pe_rows, w1cat, b1, w2, b2)


def _compress_paged(cache, layer, page_table, cw):
    DB, n_pages = page_table.shape
    assert n_pages % PAGES_PER_STEP == 0
    n_steps = n_pages // PAGES_PER_STEP
    n_ch = n_pages * PAGE_SIZE // CMP_STRIDE
    pe_rows, w1cat, b1, w2, b2 = cw
    xw = KV_UNITS * CMP_STRIDE * HEAD_DIM

    def page_spec(k):
        return pl.BlockSpec((None, None, PAGE_SIZE * KV_UNITS, HEAD_DIM),
                            lambda b, s, pt: (layer, pt[b * n_pages + s * PAGES_PER_STEP + k], 0, 0))

    grid_spec = pltpu.PrefetchScalarGridSpec(
        num_scalar_prefetch=1,
        grid=(DB, n_steps),
        in_specs=[page_spec(k) for k in range(PAGES_PER_STEP)] + [
            pl.BlockSpec(pe_rows.shape, lambda b, s, pt: (0, 0)),
            pl.BlockSpec(w1cat.shape, lambda b, s, pt: (0, 0, 0)),
            pl.BlockSpec(b1.shape, lambda b, s, pt: (0, 0, 0)),
            pl.BlockSpec(w2.shape, lambda b, s, pt: (0, 0, 0)),
            pl.BlockSpec(b2.shape, lambda b, s, pt: (0, 0, 0))],
        out_specs=pl.BlockSpec((None, n_ch, KV_UNITS * HEAD_DIM), lambda b, s, pt: (b, 0, 0)),
        scratch_shapes=[pltpu.VMEM((n_ch + 16, xw), F32)],
    )
    return pl.pallas_call(
        functools.partial(_compress_paged_kernel, n_ch=n_ch),
        grid_spec=grid_spec,
        out_shape=jax.ShapeDtypeStruct((DB, n_ch, KV_UNITS * HEAD_DIM), BF16),
        compiler_params=_params(2),
        name="compress_paged",
    )(page_table.reshape(-1), *([cache] * PAGES_PER_STEP), pe_rows, w1cat, b1, w2, b2)


def _nsa_prompt_kernel(q_ref, ks_ref, kw_ref, kcv_ref, gate_ref, ov_ref, e_ref, o_ref, *, n_cmp, n_sel):
    TQ, TK = NSA_TQ, NSA_TK
    R = Q_PER_KV * TQ
    t0 = pl.program_id(1) * TQ
    tpos = t0 + lax.broadcasted_iota(jnp.int32, (TQ, 1), 0)
    tpos4 = jnp.concatenate([tpos] * Q_PER_KV, axis=0)
    NC = kcv_ref.shape[0]
    cidx = lax.broadcasted_iota(jnp.int32, (1, NC), 1)
    m_c = (cidx * CMP_STRIDE + (CMP_BLK - 1) <= tpos4) & (cidx < n_cmp)
    klane = lax.broadcasted_iota(jnp.int32, (1, TK), 1)
    n_sel_tiles = (t0 + TQ + TK - 1) // TK
    w_lo = jnp.maximum(t0 - WINDOW, 0) // TK
    w_hi = (t0 + TQ - 1) // TK + 1

    q4s, o_cs, imps = [], [], []
    for g in range(KV_HEADS):
        q4 = jnp.concatenate([q_ref[:, (Q_PER_KV * g + r) * HEAD_DIM:(Q_PER_KV * g + r + 1) * HEAD_DIM]
                              for r in range(Q_PER_KV)], axis=0)
        p_c = _masked_softmax(_dot_nt(q4, kcv_ref[:, g * HEAD_DIM:(g + 1) * HEAD_DIM]) * ATT_SCALE, m_c)
        o_cs.append(_dot(p_c.astype(BF16), kcv_ref[:, (KV_HEADS + g) * HEAD_DIM:(KV_HEADS + g + 1) * HEAD_DIM]))
        imp4 = _split_dot(p_c, ov_ref[...])
        imp = imp4[0:TQ]
        for r in range(1, Q_PER_KV):
            imp = imp + imp4[r * TQ:(r + 1) * TQ]
        q4s.append(q4)
        imps.append(imp)
    sel_all = _select_blocks(jnp.concatenate(imps, axis=0), jnp.concatenate([tpos] * KV_HEADS, axis=0), n_sel)

    for g in range(KV_HEADS):
        q4, o_c = q4s[g], o_cs[g]
        kcol = slice(g * HEAD_DIM, (g + 1) * HEAD_DIM)
        vcol = slice((KV_HEADS + g) * HEAD_DIM, (KV_HEADS + g + 1) * HEAD_DIM)
        sel = sel_all[g * TQ:(g + 1) * TQ]
        sel4 = jnp.concatenate([sel] * Q_PER_KV, axis=0).astype(BF16)

        RS = R // NSA_ROW_SPLIT
        slabs = [slice(i * RS, (i + 1) * RS) for i in range(NSA_ROW_SPLIT)]
        init = tuple(_online_init(RS) for _ in slabs)

        def sel_step(kb, carry):
            k0 = pl.multiple_of(kb * TK, TK)
            k, v, expand = ks_ref[pl.ds(k0, TK), kcol], ks_ref[pl.ds(k0, TK), vcol], e_ref[kb]
            out = []
            for rows, c in zip(slabs, carry):
                s = _dot_nt(q4[rows], k) * ATT_SCALE
                mask = (_dot(sel4[rows], expand) > 0.5) & (k0 + klane <= tpos4[rows])
                out.append(_online_update(s, mask, v, *c))
            return tuple(out)

        done = lax.fori_loop(0, n_sel_tiles, sel_step, init)
        o_s = jnp.concatenate([_online_finish(l, acc) for _, l, acc in done], axis=0)

        def win_step(kb, carry):
            k0 = pl.multiple_of(kb * TK, TK)
            s = _dot_nt(q4, kw_ref[pl.ds(k0, TK), kcol]) * ATT_SCALE
            delta = tpos4 - (k0 + klane)
            return _online_update(s, (delta >= 0) & (delta < WINDOW), kw_ref[pl.ds(k0, TK), vcol], *carry)

        _, l_w, acc_w = lax.fori_loop(w_lo, w_hi, win_step, _online_init(R))
        o_w = _online_finish(l_w, acc_w)

        for r in range(Q_PER_KV):
            h = Q_PER_KV * g + r
            rows = slice(r * TQ, (r + 1) * TQ)
            o = (gate_ref[:, 3 * h:3 * h + 1] * o_c[rows] + gate_ref[:, 3 * h + 1:3 * h + 2] * o_s[rows]
                 + gate_ref[:, 3 * h + 2:3 * h + 3] * o_w[rows])
            o_ref[:, h * HEAD_DIM:(h + 1) * HEAD_DIM] = o.astype(o_ref.dtype)


def _nsa_prompt(q16, kv16, kcv, gates, B, T):
    TQ, TK = NSA_TQ, NSA_TK
    nq = T // TQ
    NC = kcv.shape[1]
    n_cmp, n_sel = NC - (CMP_R - 1), T // SEL_BLK
    ov = _overlap_matrix(n_cmp, n_sel, NC, LANES)
    kidx = np.arange(T) // SEL_BLK
    e = (np.arange(LANES)[None, :, None] == kidx.reshape(T // TK, 1, TK)).astype(np.float32)
    e = jnp.asarray(e, dtype=BF16)
    kv_w = 2 * KV_COLS
    return pl.pallas_call(
        functools.partial(_nsa_prompt_kernel, n_cmp=n_cmp, n_sel=n_sel),
        grid=(B, nq),
        in_specs=[pl.BlockSpec((TQ, ATT_W), lambda b, i: (b * nq + i, 0)),
                  pl.BlockSpec((T, kv_w), lambda b, i: (b, 1)),
                  pl.BlockSpec((T, kv_w), lambda b, i: (b, 2)),
                  pl.BlockSpec((None, NC, KV_UNITS * HEAD_DIM), lambda b, i: (b, 0, 0)),
                  pl.BlockSpec((TQ, LANES), lambda b, i: (b * nq + i, 0)),
                  pl.BlockSpec(ov.shape, lambda b, i: (0, 0)),
                  pl.BlockSpec(e.shape, lambda b, i: (0, 0, 0))],
        out_specs=pl.BlockSpec((TQ, ATT_W), lambda b, i: (b * nq + i, 0)),
        out_shape=jax.ShapeDtypeStruct((B * T, ATT_W), BF16),
        compiler_params=_params(2),
        name="nsa_prompt",
    )(q16, kv16, kv16, kcv, gates, ov, e)


def _unit_rows(ref, u, n):
    return ref[pl.ds(u, n, stride=KV_UNITS), :]


def _both_heads(ref, kv, n, pad_to=None):
    parts = []
    for g in range(KV_HEADS):
        parts.append(_unit_rows(ref, kv * KV_HEADS + g, n))
        if pad_to is not None and pad_to > n:
            parts.append(jnp.zeros((pad_to - n, HEAD_DIM), F32))
    return jnp.concatenate(parts, axis=0).astype(BF16)


def _log2(n):
    assert n > 0 and n & (n - 1) == 0, n
    return n.bit_length() - 1


def _div(x, n):
    return jnp.right_shift(x, _log2(n))


def _mod(x, n):
    _log2(n)
    return jnp.bitwise_and(x, n - 1)


def _nsa_sample_kernel(pt_ref, *refs, past, n_cmp, n_sel, DS):
    del pt_ref
    pages = refs[:PAGES_PER_STEP]
    (q_ref, kcv_ref, snew_ref, wst_ref, wnew_ref, gate_ref, ov_ref, o_ref,
     sel_ref, oc_ref, m_ref, l_ref, acc_ref) = refs[PAGES_PER_STEP:]
    step = pl.program_id(1)
    R = N_HEADS * DS
    RG = Q_PER_KV * DS
    ridx = lax.broadcasted_iota(jnp.int32, (R, 1), 0)
    tpos = past + _mod(ridx, DS)
    row_g = _div(ridx, RG)
    q = jnp.concatenate([q_ref[:, h * HEAD_DIM:(h + 1) * HEAD_DIM] for h in range(N_HEADS)], axis=0).astype(BF16)

    def same_head(n_lanes, per_head):
        lane = lax.broadcasted_iota(jnp.int32, (1, n_lanes), 1)
        return (_div(lane, per_head) == row_g), _mod(lane, per_head)

    @pl.when(step == 0)
    def _():
        NC = kcv_ref.shape[0]
        kc = jnp.concatenate([kcv_ref[:, g * HEAD_DIM:(g + 1) * HEAD_DIM] for g in range(KV_HEADS)], axis=0)
        vc = jnp.concatenate([kcv_ref[:, (KV_HEADS + g) * HEAD_DIM:(KV_HEADS + g + 1) * HEAD_DIM]
                              for g in range(KV_HEADS)], axis=0)
        same, cidx = same_head(KV_HEADS * NC, NC)
        m_c = same & (cidx * CMP_STRIDE + (CMP_BLK - 1) <= tpos) & (cidx < n_cmp)
        p_c = _masked_softmax(_dot_nt(q, kc) * ATT_SCALE, m_c)
        oc_ref[...] = _dot(p_c.astype(BF16), vc)
        ov2 = jnp.concatenate([ov_ref[...]] * KV_HEADS, axis=0)
        imp_all = _split_dot(p_c, ov2)
        imps = []
        for g in range(KV_HEADS):
            imp = imp_all[g * RG:g * RG + DS]
            for r in range(1, Q_PER_KV):
                imp = imp + imp_all[g * RG + r * DS:g * RG + (r + 1) * DS]
            imps.append(imp)
        sel_all = _select_blocks(jnp.concatenate(imps, axis=0), tpos[0:KV_HEADS * DS], n_sel)
        for g in range(KV_HEADS):
            sel_ref[g * RG:(g + 1) * RG, :] = jnp.concatenate([sel_all[g * DS:(g + 1) * DS]] * Q_PER_KV, axis=0)
        m0, l0, a0 = _online_init(R)
        m_ref[...] = m0
        l_ref[...] = l0
        acc_ref[...] = a0

    sel_b = sel_ref[...].astype(BF16)
    L = sel_ref.shape[1]
    jrow = lax.broadcasted_iota(jnp.int32, (L, 1), 0)
    lane = lax.broadcasted_iota(jnp.int32, (1, PAGES_PER_STEP * KV_HEADS * PAGE_SIZE), 1)
    kpos = (step * PAGES_PER_STEP + _div(lane, KV_HEADS * PAGE_SIZE)) * PAGE_SIZE + _mod(lane, PAGE_SIZE)
    same = _mod(_div(lane, PAGE_SIZE), KV_HEADS) == row_g
    expand = jnp.where(jrow == _div(kpos, SEL_BLK), 1.0, 0.0).astype(BF16)
    mask = same & (_dot(sel_b, expand) > 0.5) & (kpos <= tpos)
    k_all = jnp.concatenate([_both_heads(pg, 0, PAGE_SIZE) for pg in pages], axis=0)
    v_all = jnp.concatenate([_both_heads(pg, 1, PAGE_SIZE) for pg in pages], axis=0)
    s = _dot_nt(q, k_all) * ATT_SCALE
    m_ref[...], l_ref[...], acc_ref[...] = _online_update(s, mask, v_all, m_ref[...], l_ref[...], acc_ref[...])

    @pl.when(step == pl.num_programs(1) - 1)
    def _():
        same_n, tn = same_head(KV_HEADS * LANES, LANES)
        same_n = same_n & (tn < DS)
        npos = past + tn
        expand = jnp.where(jrow == _div(npos, SEL_BLK), 1.0, 0.0).astype(BF16)
        mask = same_n & (_dot(sel_b, expand) > 0.5) & (npos <= tpos)
        s = _dot_nt(q, _both_heads(snew_ref, 0, DS, LANES)) * ATT_SCALE
        _, l_s, acc_s = _online_update(s, mask, _both_heads(snew_ref, 1, DS, LANES),
                                       m_ref[...], l_ref[...], acc_ref[...])
        o_s = _online_finish(l_s, acc_s)

        same_w, win = same_head(KV_HEADS * WINDOW, WINDOW)
        wpos = past - WINDOW + win
        delta = tpos - wpos
        mask = same_w & (delta >= 0) & (delta < WINDOW) & (wpos >= 0)
        s = _dot_nt(q, _both_heads(wst_ref, 0, WINDOW)) * ATT_SCALE
        carry_w = _online_update(s, mask, _both_heads(wst_ref, 1, WINDOW), *_online_init(R))
        delta = tpos - npos
        mask = same_n & (delta >= 0) & (delta < WINDOW)
        s = _dot_nt(q, _both_heads(wnew_ref, 0, DS, LANES)) * ATT_SCALE
        _, l_w, acc_w = _online_update(s, mask, _both_heads(wnew_ref, 1, DS, LANES), *carry_w)
        o_w = _online_finish(l_w, acc_w)

        o_c = oc_ref[...]
        for h in range(N_HEADS):
            rows = slice(h * DS, (h + 1) * DS)
            o = (gate_ref[:, 3 * h:3 * h + 1] * o_c[rows] + gate_ref[:, 3 * h + 1:3 * h + 2] * o_s[rows]
                 + gate_ref[:, 3 * h + 2:3 * h + 3] * o_w[rows])
            o_ref[:, h * HEAD_DIM:(h + 1) * HEAD_DIM] = o


def _nsa_sample(q, kcv, sel_new, win_state, win_new, gates, cache_sel, layer, page_table):
    DB, DS, _ = q.shape
    n_pages = page_table.shape[1]
    past = n_pages * PAGE_SIZE
    assert n_pages % PAGES_PER_STEP == 0 and DS % 8 == 0 and SEL_BLK % DS == 0
    n_steps = n_pages // PAGES_PER_STEP
    NC = kcv.shape[1]
    n_cmp = (past + DS) // CMP_STRIDE - (CMP_R - 1)
    assert NC == n_cmp + (CMP_R - 1)
    n_sel = -(-(past + DS) // SEL_BLK)
    L = -(-n_sel // LANES) * LANES
    ov = _overlap_matrix(n_cmp, n_sel, NC, L)
    R = N_HEADS * DS

    def page_spec(k):
        return pl.BlockSpec((None, None, PAGE_SIZE * KV_UNITS, HEAD_DIM),
                            lambda b, s, pt: (layer, pt[b * n_pages + s * PAGES_PER_STEP + k], 0, 0))

    per_b = lambda shape: pl.BlockSpec((None,) + shape, lambda b, s, pt: (b,) + (0,) * len(shape))
    grid_spec = pltpu.PrefetchScalarGridSpec(
        num_scalar_prefetch=1,
        grid=(DB, n_steps),
        in_specs=[page_spec(k) for k in range(PAGES_PER_STEP)] + [
            per_b((DS, ATT_W)),
            per_b((NC, KV_UNITS * HEAD_DIM)),
            per_b((DS * KV_UNITS, HEAD_DIM)),
            pl.BlockSpec((None, None, WINDOW * KV_UNITS, HEAD_DIM), lambda b, s, pt: (layer, b, 0, 0)),
            per_b((DS * KV_UNITS, HEAD_DIM)),
            per_b((DS, LANES)),
            pl.BlockSpec(ov.shape, lambda b, s, pt: (0, 0))],
        out_specs=per_b((DS, ATT_W)),
        scratch_shapes=[pltpu.VMEM((R, L), F32), pltpu.VMEM((R, HEAD_DIM), F32), pltpu.VMEM((R, 1), F32),
                        pltpu.VMEM((R, 1), F32), pltpu.VMEM((R, HEAD_DIM), F32)],
    )
    return pl.pallas_call(
        functools.partial(_nsa_sample_kernel, past=past, n_cmp=n_cmp, n_sel=n_sel, DS=DS),
        grid_spec=grid_spec,
        out_shape=jax.ShapeDtypeStruct((DB, DS, ATT_W), F32),
        compiler_params=_params(2),
        name="nsa_sample",
    )(page_table.reshape(-1), *([cache_sel] * PAGES_PER_STEP), q, kcv, sel_new, win_state, win_new, gates, ov)


def _moe_up_kernel(be_ref, bv_ref, x_ref, w_ref, b_ref, pick_ref, o_ref, wbf_ref):
    i = pl.program_id(1)
    prev_e = be_ref[jnp.maximum(i - 1, 0)]

    @pl.when((i == 0) | (be_ref[i] != prev_e))
    def _():
        wbf_ref[...] = w_ref[...].astype(BF16)

    @pl.when(bv_ref[i] != 0)
    def _():
        h = _dot(x_ref[...], wbf_ref[...]) + b_ref[...]
        g = jnp.minimum(h, SWIGLU_LIMIT)
        lin = jnp.clip(h, -SWIGLU_LIMIT, SWIGLU_LIMIT) + 1.0
        t = g * jax.nn.sigmoid(SWIGLU_ALPHA * g) * pltpu.roll(lin, h.shape[1] - 1, 1)
        o_ref[...] = _dot(t.astype(BF16), pick_ref[...]).astype(o_ref.dtype)

    @pl.when(bv_ref[i] == 0)
    def _():
        o_ref[...] = jnp.zeros_like(o_ref)


def _moe_up(block_e, block_valid, xs, w_up, b_up, layer, tn):
    rows, D = xs.shape
    F = w_up.shape[3] // 2
    nb = rows // MOE_TM
    pick = jnp.asarray(np.arange(2 * tn)[:, None] == 2 * np.arange(tn)[None, :], dtype=BF16)
    grid_spec = pltpu.PrefetchScalarGridSpec(
        num_scalar_prefetch=2,
        grid=(F // tn, nb),
        in_specs=[pl.BlockSpec((MOE_TM, D), lambda j, i, be, bv: (i, 0)),
                  pl.BlockSpec((None, None, D, 2 * tn), lambda j, i, be, bv: (layer, be[i], 0, j)),
                  pl.BlockSpec((None, None, 1, 2 * tn), lambda j, i, be, bv: (layer, be[i], 0, j)),
                  pl.BlockSpec((2 * tn, tn), lambda j, i, be, bv: (0, 0))],
        out_specs=pl.BlockSpec((MOE_TM, tn), lambda j, i, be, bv: (i, j)),
        scratch_shapes=[pltpu.VMEM((D, 2 * tn), BF16)],
    )
    return pl.pallas_call(
        _moe_up_kernel,
        grid_spec=grid_spec,
        out_shape=jax.ShapeDtypeStruct((rows, F), BF16),
        compiler_params=_params(2),
        name="moe_up",
    )(block_e, block_valid, xs, w_up, b_up, pick)


def _moe_down_kernel(be_ref, bv_ref, a_ref, w_ref, b_ref, rw_ref, o_ref, wbf_ref):
    i = pl.program_id(1)
    prev_e = be_ref[jnp.maximum(i - 1, 0)]

    @pl.when((i == 0) | (be_ref[i] != prev_e))
    def _():
        wbf_ref[...] = w_ref[...].astype(BF16)

    @pl.when(bv_ref[i] != 0)
    def _():
        h = _dot(a_ref[...], wbf_ref[...]) + b_ref[...]
        o_ref[...] = h * rw_ref[...]

    @pl.when(bv_ref[i] == 0)
    def _():
        o_ref[...] = jnp.zeros_like(o_ref)


def _moe_down(block_e, block_valid, act, w_down, b_down, row_w, layer, tn):
    rows, F = act.shape
    D = w_down.shape[3]
    nb = rows // MOE_TM
    grid_spec = pltpu.PrefetchScalarGridSpec(
        num_scalar_prefetch=2,
        grid=(D // tn, nb),
        in_specs=[pl.BlockSpec((MOE_TM, F), lambda j, i, be, bv: (i, 0)),
                  pl.BlockSpec((None, None, F, tn), lambda j, i, be, bv: (layer, be[i], 0, j)),
                  pl.BlockSpec((None, None, 1, tn), lambda j, i, be, bv: (layer, be[i], 0, j)),
                  pl.BlockSpec((MOE_TM, 1), lambda j, i, be, bv: (i, 0))],
        out_specs=pl.BlockSpec((MOE_TM, tn), lambda j, i, be, bv: (i, j)),
        scratch_shapes=[pltpu.VMEM((F, tn), BF16)],
    )
    return pl.pallas_call(
        _moe_down_kernel,
        grid_spec=grid_spec,
        out_shape=jax.ShapeDtypeStruct((rows, D), F32),
        compiler_params=_params(2),
        name="moe_down",
    )(block_e, block_valid, act, w_down, b_down, row_w)


def _moe(x, router_w, router_b, w_up, b_up, w_down, b_down, layer):
    N, D = x.shape
    logits = jnp.dot(x, router_w, precision=HI) + router_b
    top_v, top_e = lax.top_k(logits, TOP_K)
    gate = jax.nn.softmax(top_v, axis=-1)
    M = N * TOP_K
    flat_e = top_e.reshape(M)
    order = jnp.argsort(flat_e).astype(jnp.int32)
    counts = jnp.sum(flat_e[:, None] == jnp.arange(N_EXPERTS, dtype=flat_e.dtype)[None, :], axis=0).astype(jnp.int32)
    padded = (counts + MOE_TM - 1) // MOE_TM * MOE_TM
    start = jnp.cumsum(counts) - counts
    pend = jnp.cumsum(padded)
    pstart = pend - padded
    nb = -(-(M + N_EXPERTS * (MOE_TM - 1)) // MOE_TM)
    rows = nb * MOE_TM
    blk0 = jnp.arange(nb, dtype=jnp.int32) * MOE_TM
    block_e = jnp.minimum(jnp.sum(pend[None, :] <= blk0[:, None], axis=1), N_EXPERTS - 1).astype(jnp.int32)
    block_valid = (blk0 < pend[-1]).astype(jnp.int32)
    p = jnp.arange(rows, dtype=jnp.int32)
    e_p = block_e[p // MOE_TM]
    q = p - pstart[e_p]
    live = q < counts[e_p]
    src = order[jnp.clip(start[e_p] + q, 0, M - 1)]
    row_tok = jnp.where(live, src // TOP_K, N).astype(jnp.int32)
    row_w = jnp.where(live, gate.reshape(M)[src], 0.0)
    slot = pstart[flat_e] + jnp.argsort(order).astype(jnp.int32) - start[flat_e]

    x_pad = jnp.concatenate([x.astype(BF16), jnp.zeros((1, D), BF16)], axis=0)
    xs = x_pad[row_tok]
    act = _moe_up(block_e, block_valid, xs, w_up, b_up.reshape(DEPTH, N_EXPERTS, 1, 2 * D_FF), layer, tn=512)
    out = _moe_down(block_e, block_valid, act, w_down, b_down.reshape(DEPTH, N_EXPERTS, 1, D), row_w[:, None], layer, tn=1024)
    return out, slot.reshape(N, TOP_K).T.reshape(M)


def _residual_ln_kernel(*refs, n_add):
    adds, (x_ref, g_ref, b_ref, o_ref) = refs[:n_add], refs[n_add:]
    y = adds[0][...]
    for a in adds[1:]:
        y = y + a[...]
    z = DN_ALPHA * x_ref[...] + y
    mu = jnp.mean(z, axis=-1, keepdims=True)
    var = jnp.mean(jnp.square(z - mu), axis=-1, keepdims=True)
    o_ref[...] = (z - mu) * lax.rsqrt(var + LN_EPS) * g_ref[...] + b_ref[...]


def _residual_ln(stacked, n_add, x, g, b):
    N, D = x.shape
    tm = 256
    nt = N // tm
    assert N % tm == 0 and stacked.shape == (n_add * N, D)
    return pl.pallas_call(
        functools.partial(_residual_ln_kernel, n_add=n_add),
        grid=(nt,),
        in_specs=[pl.BlockSpec((tm, D), functools.partial(lambda i, k: (k * nt + i, 0), k=k)) for k in range(n_add)] + [
            pl.BlockSpec((tm, D), lambda i: (i, 0)),
            pl.BlockSpec((1, D), lambda i: (0, 0)),
            pl.BlockSpec((1, D), lambda i: (0, 0))],
        out_specs=pl.BlockSpec((tm, D), lambda i: (i, 0)),
        out_shape=jax.ShapeDtypeStruct((N, D), F32),
        compiler_params=_params(1),
        name="residual_ln",
    )(*([stacked] * n_add), x, g.reshape(1, D), b.reshape(1, D))


def _row_copy(rows_hbm, buf, sem, slot_ref, n_tok, step, k, r):
    tm = buf.shape[2]
    par = step % 2
    row = slot_ref[k * n_tok + step * tm + r]
    return pltpu.make_async_copy(rows_hbm.at[pl.ds(row, 1)], buf.at[par, k, pl.ds(r, 1)], sem.at[par])


def _combine_ln_kernel(slot_ref, rows_hbm, x_ref, g_ref, b_ref, o_ref, buf, sem, *, n_tok):
    i = pl.program_id(0)
    tm = buf.shape[2]

    def for_rows(step, fn):
        def body(r, carry):
            for k in range(TOP_K):
                fn(_row_copy(rows_hbm, buf, sem, slot_ref, n_tok, step, k, r))
            return carry
        lax.fori_loop(0, tm, body, 0)

    @pl.when(i == 0)
    def _():
        for_rows(i, lambda c: c.start())

    @pl.when(i + 1 < pl.num_programs(0))
    def _():
        for_rows(i + 1, lambda c: c.start())

    for_rows(i, lambda c: c.wait())
    par = i % 2
    y = buf[par, 0]
    for k in range(1, TOP_K):
        y = y + buf[par, k]
    z = DN_ALPHA * x_ref[...] + y
    mu = jnp.mean(z, axis=-1, keepdims=True)
    var = jnp.mean(jnp.square(z - mu), axis=-1, keepdims=True)
    o_ref[...] = (z - mu) * lax.rsqrt(var + LN_EPS) * g_ref[...] + b_ref[...]


def _combine_ln(rows, slot_km, x, g, b):
    N, D = x.shape
    tm = 64
    assert N % tm == 0
    grid_spec = pltpu.PrefetchScalarGridSpec(
        num_scalar_prefetch=1,
        grid=(N // tm,),
        in_specs=[pl.BlockSpec(memory_space=pl.ANY),
                  pl.BlockSpec((tm, D), lambda i, s: (i, 0)),
                  pl.BlockSpec((1, D), lambda i, s: (0, 0)),
                  pl.BlockSpec((1, D), lambda i, s: (0, 0))],
        out_specs=pl.BlockSpec((tm, D), lambda i, s: (i, 0)),
        scratch_shapes=[pltpu.VMEM((2, TOP_K, tm, D), F32), pltpu.SemaphoreType.DMA((2,))],
    )
    return pl.pallas_call(
        functools.partial(_combine_ln_kernel, n_tok=N),
        grid_spec=grid_spec,
        out_shape=jax.ShapeDtypeStruct((N, D), F32),
        compiler_params=_params(1),
        name="combine_ln",
    )(slot_km, rows, x, g.reshape(1, D), b.reshape(1, D))


def _conv_body(u, hu, w_ref, b_ref, g_ref, beta_ref, o_ref, tail_ref, s_ref, y_ref):
    TT = u.shape[0]
    s_ref[0:CONV_HALO, :] = hu
    s_ref[CONV_HALO:CONV_HALO + TT, :] = u
    off = CONV_HALO - (CONV_W - 1)
    for cc in range(C_CONV // LANES):
        cols = slice(cc * LANES, (cc + 1) * LANES)
        acc = jnp.broadcast_to(b_ref[:, cols], (TT, LANES))
        for j in range(CONV_W):
            acc = acc + s_ref[pl.ds(j + off, TT), cols] * w_ref[j:j + 1, cols]
        y_ref[:, cols] = acc
    y = y_ref[...]
    mu = jnp.mean(y, axis=-1, keepdims=True)
    var = jnp.mean(jnp.square(y - mu), axis=-1, keepdims=True)
    yn = (y - mu) * lax.rsqrt(var + LN_EPS) * g_ref[...] + beta_ref[...]
    o_ref[...] = (yn * jax.nn.sigmoid(yn)).astype(o_ref.dtype)
    tail_ref[...] = s_ref[TT:TT + CONV_HALO, :]


def _glu(ref):
    return ref[:, :C_CONV] * jax.nn.sigmoid(ref[:, C_CONV:])


def _conv_prompt_kernel(main_ref, halo_ref, *rest):
    hu = jnp.where(pl.program_id(1) > 0, _glu(halo_ref), 0.0)
    _conv_body(_glu(main_ref), hu, *rest)


def _conv_sample_kernel(main_ref, state_ref, *rest):
    _conv_body(_glu(main_ref), state_ref[...], *rest)


def _conv_call(kern, grid, in_specs, out_rows_spec, n_seq, TT, n_rows, out_dtype, name, args):
    C = C_CONV
    const = lambda shape: pl.BlockSpec(shape, lambda b, i: (0, 0))
    return pl.pallas_call(
        kern,
        grid=grid,
        in_specs=in_specs + [const((CONV_W, C)), const((1, C)), const((1, C)), const((1, C))],
        out_specs=[out_rows_spec, pl.BlockSpec((None, CONV_HALO, C), lambda b, i: (b, 0, 0))],
        out_shape=[jax.ShapeDtypeStruct((n_rows, C), out_dtype), jax.ShapeDtypeStruct((n_seq, CONV_HALO, C), F32)],
        scratch_shapes=[pltpu.VMEM((CONV_HALO + TT, C), F32), pltpu.VMEM((TT, C), F32)],
        compiler_params=_params(2),
        name=name,
    )(*args)


def _conv_prompt(cin, B, T, conv_w, conv_b, ln_g, ln_b):
    TT = 128
    nt, hb = T // TT, TT // CONV_HALO
    C = C_CONV
    in_specs = [pl.BlockSpec((TT, 2 * C), lambda b, i: (b * nt + i, 0)),
                pl.BlockSpec((CONV_HALO, 2 * C), lambda b, i: (jnp.maximum((b * nt + i) * hb - 1, 0), 0))]
    return _conv_call(_conv_prompt_kernel, (B, nt), in_specs, pl.BlockSpec((TT, C), lambda b, i: (b * nt + i, 0)),
                      B, TT, B * T, BF16, "conv_prompt",
                      (cin, cin, conv_w, conv_b.reshape(1, C), ln_g.reshape(1, C), ln_b.reshape(1, C)))


def _conv_sample(cin, row0, DB, DS, state, conv_w, conv_b, ln_g, ln_b):
    C = C_CONV
    assert row0 % DS == 0
    in_specs = [pl.BlockSpec((DS, 2 * C), lambda b, i: (row0 // DS + b, 0)),
                pl.BlockSpec((None, CONV_HALO, C), lambda b, i: (b, 0, 0))]
    return _conv_call(_conv_sample_kernel, (DB, 1), in_specs, pl.BlockSpec((DS, C), lambda b, i: (b, 0)),
                      DB, DS, DB * DS, F32, "conv_sample",
                      (cin, state, conv_w, conv_b.reshape(1, C), ln_g.reshape(1, C), ln_b.reshape(1, C)))


def _rope_tables(pos):
    inv = ROPE_THETA ** (-jnp.arange(0, HEAD_DIM, 2, dtype=F32) / HEAD_DIM)
    ang = pos.astype(F32)[:, None] * inv[None, :]
    cos, sin = jnp.cos(ang), jnp.sin(ang)
    return jnp.concatenate([cos, cos], axis=-1), jnp.concatenate([-sin, sin], axis=-1)


def kernel(x_prompt, x_sample, cache_cmp_kv, cache_sel_kv, state_win_kv, state_conv, page_table,
           w_in, cmp_pe, cmp_w1, cmp_b1, cmp_w2, cmp_b2, conv_w, conv_b, conv_ln_g, conv_ln_b,
           w_out, ln1_g, ln1_b, router_w, router_b, exp_w_up, exp_b_up, exp_w_down, exp_b_down,
           ln2_g, ln2_b):
    B, T, _ = x_prompt.shape
    DB, DS, _ = x_sample.shape
    n_pages = page_table.shape[1]
    n_pool = cache_cmp_kv.shape[1]
    past = n_pages * PAGE_SIZE
    NP, NS = B * T, DB * DS
    pos = jnp.concatenate([jnp.tile(jnp.arange(T, dtype=jnp.int32), B),
                           jnp.tile(past + jnp.arange(DS, dtype=jnp.int32), DB)])
    cos_t, sin_t = _rope_tables(pos)
    cache_cmp = cache_cmp_kv.reshape(DEPTH, n_pool, PAGE_SIZE * KV_UNITS, HEAD_DIM)
    cache_sel = cache_sel_kv.reshape(DEPTH, n_pool, PAGE_SIZE * KV_UNITS, HEAD_DIM)
    win_state = state_win_kv.reshape(DEPTH, DB, WINDOW * KV_UNITS, HEAD_DIM)
    kv_shape = (2, KV_HEADS, HEAD_DIM)
    conv_state = jnp.pad(state_conv, ((0, 0), (0, 0), (CONV_HALO - (CONV_W - 1), 0), (0, 0)))
    x = jnp.concatenate([x_prompt.reshape(NP, D_MODEL), x_sample.reshape(NS, D_MODEL)], axis=0)
    outs = [[] for _ in range(8)]
    p4, s4 = NP * KV_UNITS, NS * KV_UNITS
    for l in range(DEPTH):
        xb = x.astype(BF16)
        w_gate = jnp.pad(w_in[l][:, QKV_COLS:QKV_COLS + GATE_COLS], ((0, 0), (0, LANES - GATE_COLS)))
        q16 = _q_proj(xb, w_in, l, cos_t, sin_t)
        kvi, kv16 = _kv_proj(xb, w_in, l, cos_t, sin_t)
        cin = _matmul(xb, w_in[l][:, QKV_COLS + GATE_COLS:], tm=ROW_TM, tn=512)
        gates = _matmul(xb, w_gate, tm=ROW_TM, tn=LANES, sigmoid=True)
        cw = _compress_weights(cmp_pe[l], cmp_w1[l], cmp_b1[l], cmp_w2[l], cmp_b2[l])
        kv_s = kvi[:, p4:p4 + s4].reshape(3, DB, DS * KV_UNITS, HEAD_DIM)

        kcv_p = _compress_prompt(kvi, B, T, cw)
        att_p = _nsa_prompt(q16, kv16, kcv_p, gates, B, T)
        conv_p, tail_p = _conv_prompt(cin, B, T, conv_w[l], conv_b[l], conv_ln_g[l], conv_ln_b[l])
        assert (past + DS) // CMP_STRIDE == past // CMP_STRIDE
        kcv_s = _compress_paged(cache_cmp, l, page_table, cw)
        att_s = _nsa_sample(q16[NP:].astype(F32).reshape(DB, DS, ATT_W), kcv_s, kv_s[1], win_state, kv_s[2],
                            gates[NP:].reshape(DB, DS, LANES), cache_sel, l, page_table)
        conv_s, tail_s = _conv_sample(cin, NP, DB, DS, conv_state[l], conv_w[l], conv_b[l], conv_ln_g[l], conv_ln_b[l])

        kv_p = kvi[:, :p4].reshape((3, B, T) + kv_shape)
        outs[0].append(kv_p[0])
        outs[1].append(kv_p[1])
        outs[2].append(kv_p[2][:, -WINDOW:])
        outs[3].append(tail_p[:, CONV_HALO - (CONV_W - 1):])
        outs[4].append(kv_s[0].reshape((DB, DS) + kv_shape))
        outs[5].append(kv_s[1].reshape((DB, DS) + kv_shape))
        outs[6].append(jnp.concatenate([state_win_kv[l], kv_s[2].reshape((DB, DS) + kv_shape)], axis=1)[:, -WINDOW:])
        outs[7].append(tail_s[:, CONV_HALO - (CONV_W - 1):])

        att = jnp.concatenate([att_p, att_s.reshape(NS, ATT_W).astype(BF16)], axis=0)
        conv_out = jnp.concatenate([conv_p, conv_s.astype(BF16)], axis=0)
        mix = _matmul(jnp.concatenate([att, conv_out], axis=1), w_out, tm=ROW_TM, tn=512, layer=l)
        x = _residual_ln(mix, 1, x, ln1_g[l], ln1_b[l])
        rows, slot_km = _moe(x, router_w[l], router_b[l], exp_w_up, exp_b_up, exp_w_down, exp_b_down, l)
        x = _combine_ln(rows, slot_km, x, ln2_g[l], ln2_b[l])

    return (x[:NP].reshape(B, T, D_MODEL), x[NP:].reshape(DB, DS, D_MODEL)) + tuple(jnp.stack(o) for o in outs)
```

```python
import functools

import jax
import jax.numpy as jnp
import numpy as np
from jax import lax
from jax.experimental import pallas as pl
from jax.experimental.pallas import tpu as pltpu

D_MODEL = 2048
DEPTH = 2
PAGE_SIZE = 128
N_HEADS = 8
HEAD_DIM = 128
KV_HEADS = 2
Q_PER_KV = N_HEADS // KV_HEADS
ATT_W = N_HEADS * HEAD_DIM
KV_COLS = KV_HEADS * HEAD_DIM
KV_UNITS = 2 * KV_HEADS
CMP_BLK = 32
CMP_STRIDE = 16
CMP_R = CMP_BLK // CMP_STRIDE
SEL_BLK = 64
N_SEL_TOP = 16
WINDOW = 512
FORCE_BONUS = 1000.0
NEG = -1e30
BELOW_NEG = -3e38
ROPE_THETA = 10000.0
ATT_SCALE = HEAD_DIM ** -0.5
C_CONV = D_MODEL - ATT_W
CONV_W = 31
QKV_COLS = ATT_W + 6 * KV_COLS
GATE_COLS = 3 * N_HEADS
N_EXPERTS = 32
TOP_K = 4
D_FF = 2048
SWIGLU_LIMIT = 7.0
SWIGLU_ALPHA = 1.702
LN_EPS = 1e-5
DN_ALPHA = (2 * DEPTH) ** 0.25

LANES = 128
VMEM_LIMIT = 56 * 1024 * 1024
MOE_TM = 256
ROW_TM = 768
NSA_TQ = 128
NSA_TK = 256
NSA_ROW_SPLIT = 2
PAGES_PER_STEP = 8
CONV_HALO = 32
HI = lax.Precision.HIGHEST
F32 = jnp.float32
BF16 = jnp.bfloat16


def _params(n_axes):
    return pltpu.CompilerParams(dimension_semantics=("arbitrary",) * n_axes, vmem_limit_bytes=VMEM_LIMIT)


def _dot(a, b):
    return jnp.dot(a, b, preferred_element_type=F32)


def _dot_nt(a, b):
    return lax.dot_general(a, b, (((1,), (1,)), ((), ())), preferred_element_type=F32)


def _split_dot(p, w):
    hi = p.astype(BF16)
    r1 = p - hi.astype(F32)
    mid = r1.astype(BF16)
    lo = (r1 - mid.astype(F32)).astype(BF16)
    return _dot(hi, w) + _dot(mid, w) + _dot(lo, w)


def _mm_kernel(a_ref, w_ref, o_ref, wbf_ref, *, sigmoid):
    @pl.when(pl.program_id(1) == 0)
    def _():
        wbf_ref[...] = w_ref[...].astype(BF16)

    h = _dot(a_ref[...], wbf_ref[...])
    o_ref[...] = (jax.nn.sigmoid(h) if sigmoid else h).astype(o_ref.dtype)


def _w_spec(w, layer, K, tn, col0=0):
    if layer is None:
        return pl.BlockSpec((K, tn), lambda j, i: (0, col0 + j))
    assert w.ndim == 3
    return pl.BlockSpec((None, K, tn), lambda j, i: (layer, 0, col0 + j))


def _matmul(a, w, tm, tn, sigmoid=False, out_dtype=F32, layer=None):
    M, K = a.shape
    N = w.shape[-1]
    assert M % tm == 0 and N % tn == 0
    return pl.pallas_call(
        functools.partial(_mm_kernel, sigmoid=sigmoid),
        grid=(N // tn, M // tm),
        in_specs=[pl.BlockSpec((tm, K), lambda j, i: (i, 0)),
                  _w_spec(w, layer, K, tn)],
        out_specs=pl.BlockSpec((tm, tn), lambda j, i: (i, j)),
        out_shape=jax.ShapeDtypeStruct((M, N), out_dtype),
        scratch_shapes=[pltpu.VMEM((K, tn), BF16)],
        compiler_params=_params(2),
        name="dense_matmul",
    )(a, w)


def _rope(x, cos, sin):
    return x * cos + pltpu.roll(x, HEAD_DIM // 2, 1) * sin


def _q_kernel(a_ref, w_ref, cos_ref, sin_ref, o16_ref, wbf_ref):
    @pl.when(pl.program_id(1) == 0)
    def _():
        wbf_ref[...] = w_ref[...].astype(BF16)

    h = _dot(a_ref[...], wbf_ref[...])
    for u in range(4):
        cols = slice(u * HEAD_DIM, (u + 1) * HEAD_DIM)
        o16_ref[:, cols] = _rope(h[:, cols], cos_ref[...], sin_ref[...]).astype(BF16)


def _kv_kernel(a_ref, w_ref, cos_ref, sin_ref, o32_ref, o16_ref, wbf_ref):
    @pl.when(pl.program_id(1) == 0)
    def _():
        wbf_ref[...] = w_ref[...].astype(BF16)

    h = _dot(a_ref[...], wbf_ref[...])
    tm = h.shape[0]
    for u in range(KV_UNITS):
        cols = slice(u * HEAD_DIM, (u + 1) * HEAD_DIM)
        y = _rope(h[:, cols], cos_ref[...], sin_ref[...]) if u < KV_HEADS else h[:, cols]
        o32_ref[pl.ds(u, tm, stride=KV_UNITS), :] = y
        o16_ref[:, cols] = y.astype(BF16)


def _proj_specs(w, layer, K, tn, col0):
    return [pl.BlockSpec((ROW_TM, K), lambda j, i: (i, 0)),
            _w_spec(w, layer, K, tn, col0),
            pl.BlockSpec((ROW_TM, HEAD_DIM), lambda j, i: (i, 0)),
            pl.BlockSpec((ROW_TM, HEAD_DIM), lambda j, i: (i, 0))]


def _q_proj(a, w_in, layer, cos_t, sin_t):
    M, K = a.shape
    N = ATT_W
    tn = 4 * HEAD_DIM
    return pl.pallas_call(
        _q_kernel,
        grid=(N // tn, M // ROW_TM),
        in_specs=_proj_specs(w_in, layer, K, tn, 0),
        out_specs=pl.BlockSpec((ROW_TM, tn), lambda j, i: (i, j)),
        out_shape=jax.ShapeDtypeStruct((M, N), BF16),
        scratch_shapes=[pltpu.VMEM((K, tn), BF16)],
        compiler_params=_params(2),
        name="q_proj",
    )(a, w_in, cos_t, sin_t)


def _kv_proj(a, w_in, layer, cos_t, sin_t):
    M, K = a.shape
    N = 6 * KV_COLS
    tn = 2 * KV_COLS
    return pl.pallas_call(
        _kv_kernel,
        grid=(N // tn, M // ROW_TM),
        in_specs=_proj_specs(w_in, layer, K, tn, ATT_W // tn),
        out_specs=[pl.BlockSpec((None, KV_UNITS * ROW_TM, HEAD_DIM), lambda j, i: (j, i, 0)),
                   pl.BlockSpec((ROW_TM, tn), lambda j, i: (i, j))],
        out_shape=[jax.ShapeDtypeStruct((N // tn, KV_UNITS * M, HEAD_DIM), F32), jax.ShapeDtypeStruct((M, N), BF16)],
        scratch_shapes=[pltpu.VMEM((K, tn), BF16)],
        compiler_params=_params(2),
        name="kv_proj",
    )(a, w_in, cos_t, sin_t)


def _masked_softmax(s, mask):
    s = jnp.where(mask, s, NEG)
    m = jnp.max(s, axis=-1, keepdims=True)
    e = jnp.where(mask, jnp.exp(s - m), 0.0)
    return e / jnp.maximum(jnp.sum(e, axis=-1, keepdims=True), 1e-30)


def _online_update(s, mask, v, m, l, acc):
    s = jnp.where(mask, s, NEG)
    m_new = jnp.maximum(m, jnp.max(s, axis=-1, keepdims=True))
    alpha = jnp.exp(m - m_new)
    p = jnp.where(mask, jnp.exp(s - m_new), 0.0)
    l = alpha * l + jnp.sum(p, axis=-1, keepdims=True)
    acc = alpha * acc + _dot(p.astype(BF16), v)
    return m_new, l, acc


def _online_init(rows):
    return (jnp.full((rows, 1), NEG, F32), jnp.zeros((rows, 1), F32), jnp.zeros((rows, HEAD_DIM), F32))


def _online_finish(l, acc):
    return acc / jnp.maximum(l, 1e-30)


def _select_blocks(imp, tpos, n_sel):
    L = imp.shape[1]
    jidx = lax.broadcasted_iota(jnp.int32, (1, L), 1)
    jf = jidx.astype(F32)
    cur = _div(tpos, SEL_BLK)
    valid = (jidx * SEL_BLK <= tpos) & (jidx < n_sel)
    forced = (jidx == 0) | (jidx == cur) | (jidx == cur - 1)
    score = jnp.where(valid, imp + jnp.where(forced, FORCE_BONUS, 0.0), NEG)
    score = jnp.where(jidx < n_sel, score, BELOW_NEG)
    picked = jnp.zeros(imp.shape, F32)
    for _ in range(min(N_SEL_TOP, n_sel)):
        m = jnp.max(score, axis=-1, keepdims=True)
        first = jnp.min(jnp.where(score == m, jf, float(L)), axis=-1, keepdims=True)
        pick = jf == first
        picked = jnp.where(pick, 1.0, picked)
        score = jnp.where(pick, BELOW_NEG, score)
    return jnp.where(valid, picked, 0.0)


def _overlap_matrix(n_cmp, n_sel, rows, cols):
    i = np.arange(rows)[:, None]
    j = np.arange(cols)[None, :]
    ov = (i * CMP_STRIDE < (j + 1) * SEL_BLK) & (i * CMP_STRIDE + CMP_BLK > j * SEL_BLK) & (i < n_cmp) & (j < n_sel)
    return jnp.asarray(ov.astype(np.float32), dtype=BF16)


def _compress_body(step, n_steps, row_refs, pe_ref, w1_ref, b1_ref, w2_ref, b2_ref, o_ref, x_ref, n_ch):
    cpr = row_refs[0].shape[0] // (KV_UNITS * CMP_STRIDE)
    cps = cpr * len(row_refs)
    for k, ref in enumerate(row_refs):
        base = pl.multiple_of(step * cps + k * cpr, 8)
        for u in range(KV_UNITS):
            for s in range(CMP_STRIDE):
                col = (u * CMP_STRIDE + s) * HEAD_DIM
                x_ref[pl.ds(base, cpr), col:col + HEAD_DIM] = ref[pl.ds(KV_UNITS * s + u, cpr, stride=KV_UNITS * CMP_STRIDE), :]

    @pl.when(step == n_steps - 1)
    def _():
        x_ref[n_ch:n_ch + 16, :] = pe_ref[...]
        width = CMP_STRIDE * HEAD_DIM
        ridx = lax.broadcasted_iota(jnp.int32, (n_ch, 1), 0)
        for u in range(KV_UNITS):
            kk = u // KV_HEADS
            acc = _dot(x_ref[:, u * width:(u + 1) * width].astype(BF16), w1_ref[kk])
            pe_term = acc[n_ch:n_ch + 1, :HEAD_DIM] + acc[n_ch + 1:n_ch + 2, HEAD_DIM:]
            z = acc[:n_ch, :HEAD_DIM] + pltpu.roll(acc[:n_ch, HEAD_DIM:], n_ch - 1, 0) + (b1_ref[kk] + pe_term)
            c = _dot(jax.nn.gelu(z).astype(BF16), w2_ref[kk]) + b2_ref[kk]
            c = jnp.where(ridx < n_ch - (CMP_R - 1), c, 0.0)
            o_ref[:, u * HEAD_DIM:(u + 1) * HEAD_DIM] = c.astype(o_ref.dtype)


def _compress_prompt_kernel(rows_ref, pe_ref, w1_ref, b1_ref, w2_ref, b2_ref, o_ref, x_ref, *, n_ch):
    _compress_body(pl.program_id(1), pl.num_programs(1), [rows_ref], pe_ref, w1_ref, b1_ref, w2_ref, b2_ref,
                   o_ref, x_ref, n_ch)


def _compress_paged_kernel(pt_ref, *refs, n_ch):
    del pt_ref
    pages = list(refs[:PAGES_PER_STEP])
    pe_ref, w1_ref, b1_ref, w2_ref, b2_ref, o_ref, x_ref = refs[PAGES_PER_STEP:]
    _compress_body(pl.program_id(1), pl.num_programs(1), pages, pe_ref, w1_ref, b1_ref, w2_ref, b2_ref,
                   o_ref, x_ref, n_ch)


def _compress_weights(pe, w1, b1, w2, b2):
    hid = w1.shape[-1]
    w1r = w1.reshape(2, CMP_R, CMP_STRIDE, HEAD_DIM, hid)
    w1cat = w1r.transpose(0, 2, 3, 1, 4).reshape(2, CMP_STRIDE * HEAD_DIM, CMP_R * hid).astype(BF16)
    per = pe.reshape(2, CMP_R, CMP_STRIDE * HEAD_DIM)
    per = jnp.broadcast_to(per[:, None], (2, KV_HEADS, CMP_R, CMP_STRIDE * HEAD_DIM))
    per = per.transpose(2, 0, 1, 3).reshape(CMP_R, KV_UNITS * CMP_STRIDE * HEAD_DIM)
    pe_rows = jnp.concatenate([per, jnp.zeros((16 - CMP_R, per.shape[1]), F32)], axis=0)
    return pe_rows, w1cat, b1.reshape(2, 1, hid), w2.astype(BF16), b2.reshape(2, 1, HEAD_DIM)


def _compress_prompt(kvi, B, T, cw):
    R4 = T * KV_UNITS
    n_ch = R4 // (KV_UNITS * CMP_STRIDE)
    n_steps = 2
    pe_rows, w1cat, b1, w2, b2 = cw
    xw = KV_UNITS * CMP_STRIDE * HEAD_DIM
    return pl.pallas_call(
        functools.partial(_compress_prompt_kernel, n_ch=n_ch),
        grid=(B, n_steps),
        in_specs=[pl.BlockSpec((None, R4 // n_steps, HEAD_DIM), lambda b, s: (0, b * n_steps + s, 0)),
                  pl.BlockSpec(pe_rows.shape, lambda b, s: (0, 0)),
                  pl.BlockSpec(w1cat.shape, lambda b, s: (0, 0, 0)),
                  pl.BlockSpec(b1.shape, lambda b, s: (0, 0, 0)),
                  pl.BlockSpec(w2.shape, lambda b, s: (0, 0, 0)),
                  pl.BlockSpec(b2.shape, lambda b, s: (0, 0, 0))],
        out_specs=pl.BlockSpec((None, n_ch, KV_UNITS * HEAD_DIM), lambda b, s: (b, 0, 0)),
        out_shape=jax.ShapeDtypeStruct((B, n_ch, KV_UNITS * HEAD_DIM), BF16),
        scratch_shapes=[pltpu.VMEM((n_ch + 16, xw), F32)],
        compiler_params=_params(2),
        name="compress_prompt",
    )(kvi, pe_rows, w1cat, b1, w2, b2)


def _compress_paged(cache, layer, page_table, cw):
    DB, n_pages = page_table.shape
    assert n_pages % PAGES_PER_STEP == 0
    n_steps = n_pages // PAGES_PER_STEP
    n_ch = n_pages * PAGE_SIZE // CMP_STRIDE
    pe_rows, w1cat, b1, w2, b2 = cw
    xw = KV_UNITS * CMP_STRIDE * HEAD_DIM

    def page_spec(k):
        return pl.BlockSpec((None, None, PAGE_SIZE * KV_UNITS, HEAD_DIM),
                            lambda b, s, pt: (layer, pt[b * n_pages + s * PAGES_PER_STEP + k], 0, 0))

    grid_spec = pltpu.PrefetchScalarGridSpec(
        num_scalar_prefetch=1,
        grid=(DB, n_steps),
        in_specs=[page_spec(k) for k in range(PAGES_PER_STEP)] + [
            pl.BlockSpec(pe_rows.shape, lambda b, s, pt: (0, 0)),
            pl.BlockSpec(w1cat.shape, lambda b, s, pt: (0, 0, 0)),
            pl.BlockSpec(b1.shape, lambda b, s, pt: (0, 0, 0)),
            pl.BlockSpec(w2.shape, lambda b, s, pt: (0, 0, 0)),
            pl.BlockSpec(b2.shape, lambda b, s, pt: (0, 0, 0))],
        out_specs=pl.BlockSpec((None, n_ch, KV_UNITS * HEAD_DIM), lambda b, s, pt: (b, 0, 0)),
        scratch_shapes=[pltpu.VMEM((n_ch + 16, xw), F32)],
    )
    return pl.pallas_call(
        functools.partial(_compress_paged_kernel, n_ch=n_ch),
        grid_spec=grid_spec,
        out_shape=jax.ShapeDtypeStruct((DB, n_ch, KV_UNITS * HEAD_DIM), BF16),
        compiler_params=_params(2),
        name="compress_paged",
    )(page_table.reshape(-1), *([cache] * PAGES_PER_STEP), pe_rows, w1cat, b1, w2, b2)


def _nsa_prompt_kernel(q_ref, ks_ref, kw_ref, kcv_ref, gate_ref, ov_ref, e_ref, o_ref, *, n_cmp, n_sel):
    TQ, TK = NSA_TQ, NSA_TK
    R = Q_PER_KV * TQ
    t0 = pl.program_id(1) * TQ
    tpos = t0 + lax.broadcasted_iota(jnp.int32, (TQ, 1), 0)
    tpos4 = jnp.concatenate([tpos] * Q_PER_KV, axis=0)
    NC = kcv_ref.shape[0]
    cidx = lax.broadcasted_iota(jnp.int32, (1, NC), 1)
    m_c = (cidx * CMP_STRIDE + (CMP_BLK - 1) <= tpos4) & (cidx < n_cmp)
    klane = lax.broadcasted_iota(jnp.int32, (1, TK), 1)
    n_sel_tiles = (t0 + TQ + TK - 1) // TK
    w_lo = jnp.maximum(t0 - WINDOW, 0) // TK
    w_hi = (t0 + TQ - 1) // TK + 1

    q4s, o_cs, imps = [], [], []
    for g in range(KV_HEADS):
        q4 = jnp.concatenate([q_ref[:, (Q_PER_KV * g + r) * HEAD_DIM:(Q_PER_KV * g + r + 1) * HEAD_DIM]
                              for r in range(Q_PER_KV)], axis=0)
        p_c = _masked_softmax(_dot_nt(q4, kcv_ref[:, g * HEAD_DIM:(g + 1) * HEAD_DIM]) * ATT_SCALE, m_c)
        o_cs.append(_dot(p_c.astype(BF16), kcv_ref[:, (KV_HEADS + g) * HEAD_DIM:(KV_HEADS + g + 1) * HEAD_DIM]))
        imp4 = _split_dot(p_c, ov_ref[...])
        imp = imp4[0:TQ]
        for r in range(1, Q_PER_KV):
            imp = imp + imp4[r * TQ:(r + 1) * TQ]
        q4s.append(q4)
        imps.append(imp)
    sel_all = _select_blocks(jnp.concatenate(imps, axis=0), jnp.concatenate([tpos] * KV_HEADS, axis=0), n_sel)

    for g in range(KV_HEADS):
        q4, o_c = q4s[g], o_cs[g]
        kcol = slice(g * HEAD_DIM, (g + 1) * HEAD_DIM)
        vcol = slice((KV_HEADS + g) * HEAD_DIM, (KV_HEADS + g + 1) * HEAD_DIM)
        sel = sel_all[g * TQ:(g + 1) * TQ]
        sel4 = jnp.concatenate([sel] * Q_PER_KV, axis=0).astype(BF16)

        RS = R // NSA_ROW_SPLIT
        slabs = [slice(i * RS, (i + 1) * RS) for i in range(NSA_ROW_SPLIT)]
        init = tuple(_online_init(RS) for _ in slabs)

        def sel_step(kb, carry):
            k0 = pl.multiple_of(kb * TK, TK)
            k, v, expand = ks_ref[pl.ds(k0, TK), kcol], ks_ref[pl.ds(k0, TK), vcol], e_ref[kb]
            out = []
            for rows, c in zip(slabs, carry):
                s = _dot_nt(q4[rows], k) * ATT_SCALE
                mask = (_dot(sel4[rows], expand) > 0.5) & (k0 + klane <= tpos4[rows])
                out.append(_online_update(s, mask, v, *c))
            return tuple(out)

        done = lax.fori_loop(0, n_sel_tiles, sel_step, init)
        o_s = jnp.concatenate([_online_finish(l, acc) for _, l, acc in done], axis=0)

        def win_step(kb, carry):
            k0 = pl.multiple_of(kb * TK, TK)
            s = _dot_nt(q4, kw_ref[pl.ds(k0, TK), kcol]) * ATT_SCALE
            delta = tpos4 - (k0 + klane)
            return _online_update(s, (delta >= 0) & (delta < WINDOW), kw_ref[pl.ds(k0, TK), vcol], *carry)

        _, l_w, acc_w = lax.fori_loop(w_lo, w_hi, win_step, _online_init(R))
        o_w = _online_finish(l_w, acc_w)

        for r in range(Q_PER_KV):
            h = Q_PER_KV * g + r
            rows = slice(r * TQ, (r + 1) * TQ)
            o = (gate_ref[:, 3 * h:3 * h + 1] * o_c[rows] + gate_ref[:, 3 * h + 1:3 * h + 2] * o_s[rows]
                 + gate_ref[:, 3 * h + 2:3 * h + 3] * o_w[rows])
            o_ref[:, h * HEAD_DIM:(h + 1) * HEAD_DIM] = o.astype(o_ref.dtype)


def _nsa_prompt(q16, kv16, kcv, gates, B, T):
    TQ, TK = NSA_TQ, NSA_TK
    nq = T // TQ
    NC = kcv.shape[1]
    n_cmp, n_sel = NC - (CMP_R - 1), T // SEL_BLK
    ov = _overlap_matrix(n_cmp, n_sel, NC, LANES)
    kidx = np.arange(T) // SEL_BLK
    e = (np.arange(LANES)[None, :, None] == kidx.reshape(T // TK, 1, TK)).astype(np.float32)
    e = jnp.asarray(e, dtype=BF16)
    kv_w = 2 * KV_COLS
    return pl.pallas_call(
        functools.partial(_nsa_prompt_kernel, n_cmp=n_cmp, n_sel=n_sel),
        grid=(B, nq),
        in_specs=[pl.BlockSpec((TQ, ATT_W), lambda b, i: (b * nq + i, 0)),
                  pl.BlockSpec((T, kv_w), lambda b, i: (b, 1)),
                  pl.BlockSpec((T, kv_w), lambda b, i: (b, 2)),
                  pl.BlockSpec((None, NC, KV_UNITS * HEAD_DIM), lambda b, i: (b, 0, 0)),
                  pl.BlockSpec((TQ, LANES), lambda b, i: (b * nq + i, 0)),
                  pl.BlockSpec(ov.shape, lambda b, i: (0, 0)),
                  pl.BlockSpec(e.shape, lambda b, i: (0, 0, 0))],
        out_specs=pl.BlockSpec((TQ, ATT_W), lambda b, i: (b * nq + i, 0)),
        out_shape=jax.ShapeDtypeStruct((B * T, ATT_W), BF16),
        compiler_params=_params(2),
        name="nsa_prompt",
    )(q16, kv16, kv16, kcv, gates, ov, e)


def _unit_rows(ref, u, n):
    return ref[pl.ds(u, n, stride=KV_UNITS), :]


def _both_heads(ref, kv, n, pad_to=None):
    parts = []
    for g in range(KV_HEADS):
        parts.append(_unit_rows(ref, kv * KV_HEADS + g, n))
        if pad_to is not None and pad_to > n:
            parts.append(jnp.zeros((pad_to - n, HEAD_DIM), F32))
    return jnp.concatenate(parts, axis=0).astype(BF16)


def _log2(n):
    assert n > 0 and n & (n - 1) == 0, n
    return n.bit_length() - 1


def _div(x, n):
    return jnp.right_shift(x, _log2(n))


def _mod(x, n):
    _log2(n)
    return jnp.bitwise_and(x, n - 1)


def _nsa_sample_kernel(pt_ref, *refs, past, n_cmp, n_sel, DS):
    del pt_ref
    pages = refs[:PAGES_PER_STEP]
    (q_ref, kcv_ref, snew_ref, wst_ref, wnew_ref, gate_ref, ov_ref, o_ref,
     sel_ref, oc_ref, m_ref, l_ref, acc_ref) = refs[PAGES_PER_STEP:]
    step = pl.program_id(1)
    R = N_HEADS * DS
    RG = Q_PER_KV * DS
    ridx = lax.broadcasted_iota(jnp.int32, (R, 1), 0)
    tpos = past + _mod(ridx, DS)
    row_g = _div(ridx, RG)
    q = jnp.concatenate([q_ref[:, h * HEAD_DIM:(h + 1) * HEAD_DIM] for h in range(N_HEADS)], axis=0).astype(BF16)

    def same_head(n_lanes, per_head):
        lane = lax.broadcasted_iota(jnp.int32, (1, n_lanes), 1)
        return (_div(lane, per_head) == row_g), _mod(lane, per_head)

    @pl.when(step == 0)
    def _():
        NC = kcv_ref.shape[0]
        kc = jnp.concatenate([kcv_ref[:, g * HEAD_DIM:(g + 1) * HEAD_DIM] for g in range(KV_HEADS)], axis=0)
        vc = jnp.concatenate([kcv_ref[:, (KV_HEADS + g) * HEAD_DIM:(KV_HEADS + g + 1) * HEAD_DIM]
                              for g in range(KV_HEADS)], axis=0)
        same, cidx = same_head(KV_HEADS * NC, NC)
        m_c = same & (cidx * CMP_STRIDE + (CMP_BLK - 1) <= tpos) & (cidx < n_cmp)
        p_c = _masked_softmax(_dot_nt(q, kc) * ATT_SCALE, m_c)
        oc_ref[...] = _dot(p_c.astype(BF16), vc)
        ov2 = jnp.concatenate([ov_ref[...]] * KV_HEADS, axis=0)
        imp_all = _split_dot(p_c, ov2)
        imps = []
        for g in range(KV_HEADS):
            imp = imp_all[g * RG:g * RG + DS]
            for r in range(1, Q_PER_KV):
                imp = imp + imp_all[g * RG + r * DS:g * RG + (r + 1) * DS]
            imps.append(imp)
        sel_all = _select_blocks(jnp.concatenate(imps, axis=0), tpos[0:KV_HEADS * DS], n_sel)
        for g in range(KV_HEADS):
            sel_ref[g * RG:(g + 1) * RG, :] = jnp.concatenate([sel_all[g * DS:(g + 1) * DS]] * Q_PER_KV, axis=0)
        m0, l0, a0 = _online_init(R)
        m_ref[...] = m0
        l_ref[...] = l0
        acc_ref[...] = a0

    sel_b = sel_ref[...].astype(BF16)
    L = sel_ref.shape[1]
    jrow = lax.broadcasted_iota(jnp.int32, (L, 1), 0)
    lane = lax.broadcasted_iota(jnp.int32, (1, PAGES_PER_STEP * KV_HEADS * PAGE_SIZE), 1)
    kpos = (step * PAGES_PER_STEP + _div(lane, KV_HEADS * PAGE_SIZE)) * PAGE_SIZE + _mod(lane, PAGE_SIZE)
    same = _mod(_div(lane, PAGE_SIZE), KV_HEADS) == row_g
    expand = jnp.where(jrow == _div(kpos, SEL_BLK), 1.0, 0.0).astype(BF16)
    mask = same & (_dot(sel_b, expand) > 0.5) & (kpos <= tpos)
    k_all = jnp.concatenate([_both_heads(pg, 0, PAGE_SIZE) for pg in pages], axis=0)
    v_all = jnp.concatenate([_both_heads(pg, 1, PAGE_SIZE) for pg in pages], axis=0)
    s = _dot_nt(q, k_all) * ATT_SCALE
    m_ref[...], l_ref[...], acc_ref[...] = _online_update(s, mask, v_all, m_ref[...], l_ref[...], acc_ref[...])

    @pl.when(step == pl.num_programs(1) - 1)
    def _():
        same_n, tn = same_head(KV_HEADS * LANES, LANES)
        same_n = same_n & (tn < DS)
        npos = past + tn
        expand = jnp.where(jrow == _div(npos, SEL_BLK), 1.0, 0.0).astype(BF16)
        mask = same_n & (_dot(sel_b, expand) > 0.5) & (npos <= tpos)
        s = _dot_nt(q, _both_heads(snew_ref, 0, DS, LANES)) * ATT_SCALE
        _, l_s, acc_s = _online_update(s, mask, _both_heads(snew_ref, 1, DS, LANES),
                                       m_ref[...], l_ref[...], acc_ref[...])
        o_s = _online_finish(l_s, acc_s)

        same_w, win = same_head(KV_HEADS * WINDOW, WINDOW)
        wpos = past - WINDOW + win
        delta = tpos - wpos
        mask = same_w & (delta >= 0) & (delta < WINDOW) & (wpos >= 0)
        s = _dot_nt(q, _both_heads(wst_ref, 0, WINDOW)) * ATT_SCALE
        carry_w = _online_update(s, mask, _both_heads(wst_ref, 1, WINDOW), *_online_init(R))
        delta = tpos - npos
        mask = same_n & (delta >= 0) & (delta < WINDOW)
        s = _dot_nt(q, _both_heads(wnew_ref, 0, DS, LANES)) * ATT_SCALE
        _, l_w, acc_w = _online_update(s, mask, _both_heads(wnew_ref, 1, DS, LANES), *carry_w)
        o_w = _online_finish(l_w, acc_w)

        o_c = oc_ref[...]
        for h in range(N_HEADS):
            rows = slice(h * DS, (h + 1) * DS)
            o = (gate_ref[:, 3 * h:3 * h + 1] * o_c[rows] + gate_ref[:, 3 * h + 1:3 * h + 2] * o_s[rows]
                 + gate_ref[:, 3 * h + 2:3 * h + 3] * o_w[rows])
            o_ref[:, h * HEAD_DIM:(h + 1) * HEAD_DIM] = o


def _nsa_sample(q, kcv, sel_new, win_state, win_new, gates, cache_sel, layer, page_table):
    DB, DS, _ = q.shape
    n_pages = page_table.shape[1]
    past = n_pages * PAGE_SIZE
    assert n_pages % PAGES_PER_STEP == 0 and DS % 8 == 0 and SEL_BLK % DS == 0
    n_steps = n_pages // PAGES_PER_STEP
    NC = kcv.shape[1]
    n_cmp = (past + DS) // CMP_STRIDE - (CMP_R - 1)
    assert NC == n_cmp + (CMP_R - 1)
    n_sel = -(-(past + DS) // SEL_BLK)
    L = -(-n_sel // LANES) * LANES
    ov = _overlap_matrix(n_cmp, n_sel, NC, L)
    R = N_HEADS * DS

    def page_spec(k):
        return pl.BlockSpec((None, None, PAGE_SIZE * KV_UNITS, HEAD_DIM),
                            lambda b, s, pt: (layer, pt[b * n_pages + s * PAGES_PER_STEP + k], 0, 0))

    per_b = lambda shape: pl.BlockSpec((None,) + shape, lambda b, s, pt: (b,) + (0,) * len(shape))
    grid_spec = pltpu.PrefetchScalarGridSpec(
        num_scalar_prefetch=1,
        grid=(DB, n_steps),
        in_specs=[page_spec(k) for k in range(PAGES_PER_STEP)] + [
            per_b((DS, ATT_W)),
            per_b((NC, KV_UNITS * HEAD_DIM)),
            per_b((DS * KV_UNITS, HEAD_DIM)),
            pl.BlockSpec((None, None, WINDOW * KV_UNITS, HEAD_DIM), lambda b, s, pt: (layer, b, 0, 0)),
            per_b((DS * KV_UNITS, HEAD_DIM)),
            per_b((DS, LANES)),
            pl.BlockSpec(ov.shape, lambda b, s, pt: (0, 0))],
        out_specs=per_b((DS, ATT_W)),
        scratch_shapes=[pltpu.VMEM((R, L), F32), pltpu.VMEM((R, HEAD_DIM), F32), pltpu.VMEM((R, 1), F32),
                        pltpu.VMEM((R, 1), F32), pltpu.VMEM((R, HEAD_DIM), F32)],
    )
    return pl.pallas_call(
        functools.partial(_nsa_sample_kernel, past=past, n_cmp=n_cmp, n_sel=n_sel, DS=DS),
        grid_spec=grid_spec,
        out_shape=jax.ShapeDtypeStruct((DB, DS, ATT_W), F32),
        compiler_params=_params(2),
        name="nsa_sample",
    )(page_table.reshape(-1), *([cache_sel] * PAGES_PER_STEP), q, kcv, sel_new, win_state, win_new, gates, ov)


def _moe_up_kernel(be_ref, bv_ref, x_ref, w_ref, b_ref, pick_ref, o_ref, wbf_ref):
    i = pl.program_id(1)
    prev_e = be_ref[jnp.maximum(i - 1, 0)]

    @pl.when((i == 0) | (be_ref[i] != prev_e))
    def _():
        wbf_ref[...] = w_ref[...].astype(BF16)

    @pl.when(bv_ref[i] != 0)
    def _():
        h = _dot(x_ref[...], wbf_ref[...]) + b_ref[...]
        g = jnp.minimum(h, SWIGLU_LIMIT)
        lin = jnp.clip(h, -SWIGLU_LIMIT, SWIGLU_LIMIT) + 1.0
        t = g * jax.nn.sigmoid(SWIGLU_ALPHA * g) * pltpu.roll(lin, h.shape[1] - 1, 1)
        o_ref[...] = _dot(t.astype(BF16), pick_ref[...]).astype(o_ref.dtype)

    @pl.when(bv_ref[i] == 0)
    def _():
        o_ref[...] = jnp.zeros_like(o_ref)


def _moe_up(block_e, block_valid, xs, w_up, b_up, layer, tn):
    rows, D = xs.shape
    F = w_up.shape[3] // 2
    nb = rows // MOE_TM
    pick = jnp.asarray(np.arange(2 * tn)[:, None] == 2 * np.arange(tn)[None, :], dtype=BF16)
    grid_spec = pltpu.PrefetchScalarGridSpec(
        num_scalar_prefetch=2,
        grid=(F // tn, nb),
        in_specs=[pl.BlockSpec((MOE_TM, D), lambda j, i, be, bv: (i, 0)),
                  pl.BlockSpec((None, None, D, 2 * tn), lambda j, i, be, bv: (layer, be[i], 0, j)),
                  pl.BlockSpec((None, None, 1, 2 * tn), lambda j, i, be, bv: (layer, be[i], 0, j)),
                  pl.BlockSpec((2 * tn, tn), lambda j, i, be, bv: (0, 0))],
        out_specs=pl.BlockSpec((MOE_TM, tn), lambda j, i, be, bv: (i, j)),
        scratch_shapes=[pltpu.VMEM((D, 2 * tn), BF16)],
    )
    return pl.pallas_call(
        _moe_up_kernel,
        grid_spec=grid_spec,
        out_shape=jax.ShapeDtypeStruct((rows, F), BF16),
        compiler_params=_params(2),
        name="moe_up",
    )(block_e, block_valid, xs, w_up, b_up, pick)


def _moe_down_kernel(be_ref, bv_ref, a_ref, w_ref, b_ref, rw_ref, o_ref, wbf_ref):
    i = pl.program_id(1)
    prev_e = be_ref[jnp.maximum(i - 1, 0)]

    @pl.when((i == 0) | (be_ref[i] != prev_e))
    def _():
        wbf_ref[...] = w_ref[...].astype(BF16)

    @pl.when(bv_ref[i] != 0)
    def _():
        h = _dot(a_ref[...], wbf_ref[...]) + b_ref[...]
        o_ref[...] = h * rw_ref[...]

    @pl.when(bv_ref[i] == 0)
    def _():
        o_ref[...] = jnp.zeros_like(o_ref)


def _moe_down(block_e, block_valid, act, w_down, b_down, row_w, layer, tn):
    rows, F = act.shape
    D = w_down.shape[3]
    nb = rows // MOE_TM
    grid_spec = pltpu.PrefetchScalarGridSpec(
        num_scalar_prefetch=2,
        grid=(D // tn, nb),
        in_specs=[pl.BlockSpec((MOE_TM, F), lambda j, i, be, bv: (i, 0)),
                  pl.BlockSpec((None, None, F, tn), lambda j, i, be, bv: (layer, be[i], 0, j)),
                  pl.BlockSpec((None, None, 1, tn), lambda j, i, be, bv: (layer, be[i], 0, j)),
                  pl.BlockSpec((MOE_TM, 1), lambda j, i, be, bv: (i, 0))],
        out_specs=pl.BlockSpec((MOE_TM, tn), lambda j, i, be, bv: (i, j)),
        scratch_shapes=[pltpu.VMEM((F, tn), BF16)],
    )
    return pl.pallas_call(
        _moe_down_kernel,
        grid_spec=grid_spec,
        out_shape=jax.ShapeDtypeStruct((rows, D), F32),
        compiler_params=_params(2),
        name="moe_down",
    )(block_e, block_valid, act, w_down, b_down, row_w)


def _moe(x, router_w, router_b, w_up, b_up, w_down, b_down, layer):
    N, D = x.shape
    logits = jnp.dot(x, router_w, precision=HI) + router_b
    top_v, top_e = lax.top_k(logits, TOP_K)
    gate = jax.nn.softmax(top_v, axis=-1)
    M = N * TOP_K
    flat_e = top_e.reshape(M)
    order = jnp.argsort(flat_e).astype(jnp.int32)
    counts = jnp.sum(flat_e[:, None] == jnp.arange(N_EXPERTS, dtype=flat_e.dtype)[None, :], axis=0).astype(jnp.int32)
    padded = (counts + MOE_TM - 1) // MOE_TM * MOE_TM
    start = jnp.cumsum(counts) - counts
    pend = jnp.cumsum(padded)
    pstart = pend - padded
    nb = -(-(M + N_EXPERTS * (MOE_TM - 1)) // MOE_TM)
    rows = nb * MOE_TM
    blk0 = jnp.arange(nb, dtype=jnp.int32) * MOE_TM
    block_e = jnp.minimum(jnp.sum(pend[None, :] <= blk0[:, None], axis=1), N_EXPERTS - 1).astype(jnp.int32)
    block_valid = (blk0 < pend[-1]).astype(jnp.int32)
    p = jnp.arange(rows, dtype=jnp.int32)
    e_p = block_e[p // MOE_TM]
    q = p - pstart[e_p]
    live = q < counts[e_p]
    src = order[jnp.clip(start[e_p] + q, 0, M - 1)]
    row_tok = (src // TOP_K).astype(jnp.int32)
    row_w = jnp.where(live, gate.reshape(M)[src], 0.0)
    slot = pstart[flat_e] + jnp.argsort(order).astype(jnp.int32) - start[flat_e]

    xs = _dispatch(x, row_tok)
    act = _moe_up(block_e, block_valid, xs, w_up, b_up.reshape(DEPTH, N_EXPERTS, 1, 2 * D_FF), layer, tn=512)
    out = _moe_down(block_e, block_valid, act, w_down, b_down.reshape(DEPTH, N_EXPERTS, 1, D), row_w[:, None], layer, tn=1024)
    return out, slot.reshape(N, TOP_K).T.reshape(M)


def _residual_ln_kernel(*refs, n_add):
    adds, (x_ref, g_ref, b_ref, o_ref) = refs[:n_add], refs[n_add:]
    y = adds[0][...]
    for a in adds[1:]:
        y = y + a[...]
    z = DN_ALPHA * x_ref[...] + y
    mu = jnp.mean(z, axis=-1, keepdims=True)
    var = jnp.mean(jnp.square(z - mu), axis=-1, keepdims=True)
    o_ref[...] = (z - mu) * lax.rsqrt(var + LN_EPS) * g_ref[...] + b_ref[...]


def _residual_ln(stacked, n_add, x, g, b):
    N, D = x.shape
    tm = 256
    nt = N // tm
    assert N % tm == 0 and stacked.shape == (n_add * N, D)
    return pl.pallas_call(
        functools.partial(_residual_ln_kernel, n_add=n_add),
        grid=(nt,),
        in_specs=[pl.BlockSpec((tm, D), functools.partial(lambda i, k: (k * nt + i, 0), k=k)) for k in range(n_add)] + [
            pl.BlockSpec((tm, D), lambda i: (i, 0)),
            pl.BlockSpec((1, D), lambda i: (0, 0)),
            pl.BlockSpec((1, D), lambda i: (0, 0))],
        out_specs=pl.BlockSpec((tm, D), lambda i: (i, 0)),
        out_shape=jax.ShapeDtypeStruct((N, D), F32),
        compiler_params=_params(1),
        name="residual_ln",
    )(*([stacked] * n_add), x, g.reshape(1, D), b.reshape(1, D))


def _row_copy(rows_hbm, buf, sem, slot_ref, n_tok, step, k, r):
    tm = buf.shape[2]
    par = step % 2
    row = slot_ref[k * n_tok + step * tm + r]
    return pltpu.make_async_copy(rows_hbm.at[pl.ds(row, 1)], buf.at[par, k, pl.ds(r, 1)], sem.at[par])


def _combine_ln_kernel(slot_ref, rows_hbm, x_ref, g_ref, b_ref, o_ref, buf, sem, *, n_tok):
    i = pl.program_id(0)
    tm = buf.shape[2]

    def for_rows(step, fn):
        def body(r, carry):
            for k in range(TOP_K):
                fn(_row_copy(rows_hbm, buf, sem, slot_ref, n_tok, step, k, r))
            return carry
        lax.fori_loop(0, tm, body, 0)

    @pl.when(i == 0)
    def _():
        for_rows(i, lambda c: c.start())

    @pl.when(i + 1 < pl.num_programs(0))
    def _():
        for_rows(i + 1, lambda c: c.start())

    for_rows(i, lambda c: c.wait())
    par = i % 2
    y = buf[par, 0]
    for k in range(1, TOP_K):
        y = y + buf[par, k]
    z = DN_ALPHA * x_ref[...] + y
    mu = jnp.mean(z, axis=-1, keepdims=True)
    var = jnp.mean(jnp.square(z - mu), axis=-1, keepdims=True)
    o_ref[...] = (z - mu) * lax.rsqrt(var + LN_EPS) * g_ref[...] + b_ref[...]


def _combine_ln(rows, slot_km, x, g, b):
    N, D = x.shape
    tm = 64
    assert N % tm == 0
    grid_spec = pltpu.PrefetchScalarGridSpec(
        num_scalar_prefetch=1,
        grid=(N // tm,),
        in_specs=[pl.BlockSpec(memory_space=pl.ANY),
                  pl.BlockSpec((tm, D), lambda i, s: (i, 0)),
                  pl.BlockSpec((1, D), lambda i, s: (0, 0)),
                  pl.BlockSpec((1, D), lambda i, s: (0, 0))],
        out_specs=pl.BlockSpec((tm, D), lambda i, s: (i, 0)),
        scratch_shapes=[pltpu.VMEM((2, TOP_K, tm, D), F32), pltpu.SemaphoreType.DMA((2,))],
    )
    return pl.pallas_call(
        functools.partial(_combine_ln_kernel, n_tok=N),
        grid_spec=grid_spec,
        out_shape=jax.ShapeDtypeStruct((N, D), F32),
        compiler_params=_params(1),
        name="combine_ln",
    )(slot_km, rows, x, g.reshape(1, D), b.reshape(1, D))


def _dispatch_kernel(tok_ref, x_hbm, o_ref, buf, sem):
    i = pl.program_id(0)
    tm = buf.shape[1]

    def for_rows(step, fn):
        par = step % 2

        def body(r, carry):
            fn(pltpu.make_async_copy(x_hbm.at[pl.ds(tok_ref[step * tm + r], 1)], buf.at[par, pl.ds(r, 1)], sem.at[par]))
            return carry
        lax.fori_loop(0, tm, body, 0)

    @pl.when(i == 0)
    def _():
        for_rows(i, lambda c: c.start())

    @pl.when(i + 1 < pl.num_programs(0))
    def _():
        for_rows(i + 1, lambda c: c.start())

    for_rows(i, lambda c: c.wait())
    o_ref[...] = buf[i % 2].astype(o_ref.dtype)


def _dispatch(x, row_tok):
    N, D = x.shape
    rows = row_tok.shape[0]
    assert rows % MOE_TM == 0
    grid_spec = pltpu.PrefetchScalarGridSpec(
        num_scalar_prefetch=1,
        grid=(rows // MOE_TM,),
        in_specs=[pl.BlockSpec(memory_space=pl.ANY)],
        out_specs=pl.BlockSpec((MOE_TM, D), lambda i, t: (i, 0)),
        scratch_shapes=[pltpu.VMEM((2, MOE_TM, D), F32), pltpu.SemaphoreType.DMA((2,))],
    )
    return pl.pallas_call(
        _dispatch_kernel,
        grid_spec=grid_spec,
        out_shape=jax.ShapeDtypeStruct((rows, D), BF16),
        compiler_params=_params(1),
        name="moe_dispatch",
    )(row_tok, x)


def _conv_body(u, hu, w_ref, b_ref, g_ref, beta_ref, o_ref, tail_ref, s_ref, y_ref):
    TT = u.shape[0]
    s_ref[0:CONV_HALO, :] = hu
    s_ref[CONV_HALO:CONV_HALO + TT, :] = u
    off = CONV_HALO - (CONV_W - 1)
    for cc in range(C_CONV // LANES):
        cols = slice(cc * LANES, (cc + 1) * LANES)
        acc = jnp.broadcast_to(b_ref[:, cols], (TT, LANES))
        for j in range(CONV_W):
            acc = acc + s_ref[pl.ds(j + off, TT), cols] * w_ref[j:j + 1, cols]
        y_ref[:, cols] = acc
    y = y_ref[...]
    mu = jnp.mean(y, axis=-1, keepdims=True)
    var = jnp.mean(jnp.square(y - mu), axis=-1, keepdims=True)
    yn = (y - mu) * lax.rsqrt(var + LN_EPS) * g_ref[...] + beta_ref[...]
    o_ref[...] = (yn * jax.nn.sigmoid(yn)).astype(o_ref.dtype)
    tail_ref[...] = s_ref[TT:TT + CONV_HALO, :]


def _glu(ref):
    return ref[:, :C_CONV] * jax.nn.sigmoid(ref[:, C_CONV:])


def _conv_prompt_kernel(main_ref, halo_ref, *rest):
    hu = jnp.where(pl.program_id(1) > 0, _glu(halo_ref), 0.0)
    _conv_body(_glu(main_ref), hu, *rest)


def _conv_sample_kernel(main_ref, state_ref, *rest):
    _conv_body(_glu(main_ref), state_ref[...], *rest)


def _conv_call(kern, grid, in_specs, out_rows_spec, n_seq, TT, n_rows, out_dtype, name, args):
    C = C_CONV
    const = lambda shape: pl.BlockSpec(shape, lambda b, i: (0, 0))
    return pl.pallas_call(
        kern,
        grid=grid,
        in_specs=in_specs + [const((CONV_W, C)), const((1, C)), const((1, C)), const((1, C))],
        out_specs=[out_rows_spec, pl.BlockSpec((None, CONV_HALO, C), lambda b, i: (b, 0, 0))],
        out_shape=[jax.ShapeDtypeStruct((n_rows, C), out_dtype), jax.ShapeDtypeStruct((n_seq, CONV_HALO, C), F32)],
        scratch_shapes=[pltpu.VMEM((CONV_HALO + TT, C), F32), pltpu.VMEM((TT, C), F32)],
        compiler_params=_params(2),
        name=name,
    )(*args)


def _conv_prompt(cin, B, T, conv_w, conv_b, ln_g, ln_b):
    TT = 128
    nt, hb = T // TT, TT // CONV_HALO
    C = C_CONV
    in_specs = [pl.BlockSpec((TT, 2 * C), lambda b, i: (b * nt + i, 0)),
                pl.BlockSpec((CONV_HALO, 2 * C), lambda b, i: (jnp.maximum((b * nt + i) * hb - 1, 0), 0))]
    return _conv_call(_conv_prompt_kernel, (B, nt), in_specs, pl.BlockSpec((TT, C), lambda b, i: (b * nt + i, 0)),
                      B, TT, B * T, BF16, "conv_prompt",
                      (cin, cin, conv_w, conv_b.reshape(1, C), ln_g.reshape(1, C), ln_b.reshape(1, C)))


def _conv_sample(cin, row0, DB, DS, state, conv_w, conv_b, ln_g, ln_b):
    C = C_CONV
    assert row0 % DS == 0
    in_specs = [pl.BlockSpec((DS, 2 * C), lambda b, i: (row0 // DS + b, 0)),
                pl.BlockSpec((None, CONV_HALO, C), lambda b, i: (b, 0, 0))]
    return _conv_call(_conv_sample_kernel, (DB, 1), in_specs, pl.BlockSpec((DS, C), lambda b, i: (b, 0)),
                      DB, DS, DB * DS, F32, "conv_sample",
                      (cin, state, conv_w, conv_b.reshape(1, C), ln_g.reshape(1, C), ln_b.reshape(1, C)))


def _rope_tables(pos):
    inv = ROPE_THETA ** (-jnp.arange(0, HEAD_DIM, 2, dtype=F32) / HEAD_DIM)
    ang = pos.astype(F32)[:, None] * inv[None, :]
    cos, sin = jnp.cos(ang), jnp.sin(ang)
    return jnp.concatenate([cos, cos], axis=-1), jnp.concatenate([-sin, sin], axis=-1)


def kernel(x_prompt, x_sample, cache_cmp_kv, cache_sel_kv, state_win_kv, state_conv, page_table,
           w_in, cmp_pe, cmp_w1, cmp_b1, cmp_w2, cmp_b2, conv_w, conv_b, conv_ln_g, conv_ln_b,
           w_out, ln1_g, ln1_b, router_w, router_b, exp_w_up, exp_b_up, exp_w_down, exp_b_down,
           ln2_g, ln2_b):
    B, T, _ = x_prompt.shape
    DB, DS, _ = x_sample.shape
    n_pages = page_table.shape[1]
    n_pool = cache_cmp_kv.shape[1]
    past = n_pages * PAGE_SIZE
    NP, NS = B * T, DB * DS
    pos = jnp.concatenate([jnp.tile(jnp.arange(T, dtype=jnp.int32), B),
                           jnp.tile(past + jnp.arange(DS, dtype=jnp.int32), DB)])
    cos_t, sin_t = _rope_tables(pos)
    cache_cmp = cache_cmp_kv.reshape(DEPTH, n_pool, PAGE_SIZE * KV_UNITS, HEAD_DIM)
    cache_sel = cache_sel_kv.reshape(DEPTH, n_pool, PAGE_SIZE * KV_UNITS, HEAD_DIM)
    win_state = state_win_kv.reshape(DEPTH, DB, WINDOW * KV_UNITS, HEAD_DIM)
    kv_shape = (2, KV_HEADS, HEAD_DIM)
    conv_state = jnp.pad(state_conv, ((0, 0), (0, 0), (CONV_HALO - (CONV_W - 1), 0), (0, 0)))
    x = jnp.concatenate([x_prompt.reshape(NP, D_MODEL), x_sample.reshape(NS, D_MODEL)], axis=0)
    outs = [[] for _ in range(8)]
    p4, s4 = NP * KV_UNITS, NS * KV_UNITS
    for l in range(DEPTH):
        xb = x.astype(BF16)
        w_gate = jnp.pad(w_in[l][:, QKV_COLS:QKV_COLS + GATE_COLS], ((0, 0), (0, LANES - GATE_COLS)))
        q16 = _q_proj(xb, w_in, l, cos_t, sin_t)
        kvi, kv16 = _kv_proj(xb, w_in, l, cos_t, sin_t)
        cin = _matmul(xb, w_in[l][:, QKV_COLS + GATE_COLS:], tm=ROW_TM, tn=512)
        gates = _matmul(xb, w_gate, tm=ROW_TM, tn=LANES, sigmoid=True)
        cw = _compress_weights(cmp_pe[l], cmp_w1[l], cmp_b1[l], cmp_w2[l], cmp_b2[l])
        kv_s = kvi[:, p4:p4 + s4].reshape(3, DB, DS * KV_UNITS, HEAD_DIM)

        kcv_p = _compress_prompt(kvi, B, T, cw)
        att_p = _nsa_prompt(q16, kv16, kcv_p, gates, B, T)
        conv_p, tail_p = _conv_prompt(cin, B, T, conv_w[l], conv_b[l], conv_ln_g[l], conv_ln_b[l])
        assert (past + DS) // CMP_STRIDE == past // CMP_STRIDE
        kcv_s = _compress_paged(cache_cmp, l, page_table, cw)
        att_s = _nsa_sample(q16[NP:].astype(F32).reshape(DB, DS, ATT_W), kcv_s, kv_s[1], win_state, kv_s[2],
                            gates[NP:].reshape(DB, DS, LANES), cache_sel, l, page_table)
        conv_s, tail_s = _conv_sample(cin, NP, DB, DS, conv_state[l], conv_w[l], conv_b[l], conv_ln_g[l], conv_ln_b[l])

        kv_p = kvi[:, :p4].reshape((3, B, T) + kv_shape)
        outs[0].append(kv_p[0])
        outs[1].append(kv_p[1])
        outs[2].append(kv_p[2][:, -WINDOW:])
        outs[3].append(tail_p[:, CONV_HALO - (CONV_W - 1):])
        outs[4].append(kv_s[0].reshape((DB, DS) + kv_shape))
        outs[5].append(kv_s[1].reshape((DB, DS) + kv_shape))
        outs[6].append(jnp.concatenate([state_win_kv[l], kv_s[2].reshape((DB, DS) + kv_shape)], axis=1)[:, -WINDOW:])
        outs[7].append(tail_s[:, CONV_HALO - (CONV_W - 1):])

        att = jnp.concatenate([att_p, att_s.reshape(NS, ATT_W).astype(BF16)], axis=0)
        conv_out = jnp.concatenate([conv_p, conv_s.astype(BF16)], axis=0)
        mix = _matmul(jnp.concatenate([att, conv_out], axis=1), w_out, tm=ROW_TM, tn=512, layer=l)
        x = _residual_ln(mix, 1, x, ln1_g[l], ln1_b[l])
        rows, slot_km = _moe(x, router_w[l], router_b[l], exp_w_up, exp_b_up, exp_w_down, exp_b_down, l)
        x = _combine_ln(rows, slot_km, x, ln2_g[l], ln2_b[l])

    return (x[:NP].reshape(B, T, D_MODEL), x[NP:].reshape(DB, DS, D_MODEL)) + tuple(jnp.stack(o) for o in outs)
```
